```python
import math
import jax, jax.numpy as jnp
from jax import lax
import numpy as np

D_MODEL = 1024
BATCH = 8
SEQ = 2048
DEPTH = 2

CHUNK = 64

W_A = 1024
N_BLOCKS_A = 16
BS_A = W_A // N_BLOCKS_A
CONV_A = 4
LRU_C = 8.0
W_B = 1024
CONV_B = 31
N_HEADS_C = 16
HEAD_DIM_C = 64
W_C = N_HEADS_C * HEAD_DIM_C
Q_BLOCK = 128
N_BRANCH = 3

DEEPNORM_ALPHA = float((2 * DEPTH) ** 0.25)
DEEPNORM_BETA = float((8 * DEPTH) ** -0.25)
LN_EPS = 1e-5

SPLIT_SIZES = [W_A, W_A, 2 * W_B, W_B, W_C, W_C, W_C, N_HEADS_C, W_C, N_BRANCH * D_MODEL]
SPLIT_IDX = [int(v) for v in np.cumsum(SPLIT_SIZES)[:-1]]
N_IN = int(sum(SPLIT_SIZES))
FORGET_OFFSET = int(sum(SPLIT_SIZES[:7]))

kernel_name = "hybrid_rglru_conformer_fox_block"


def layer_norm(x, g, b):
    xf = x.astype(jnp.float32)
    mu = jnp.mean(xf, axis=-1, keepdims=True)
    var = jnp.mean(jnp.square(xf - mu), axis=-1, keepdims=True)
    y = (xf - mu) * lax.rsqrt(var + LN_EPS)
    return (y * g.astype(jnp.float32) + b.astype(jnp.float32)).astype(x.dtype)


def causal_dwconv(x, w, b):
    k = w.shape[0]
    c = x.shape[-1]
    y = lax.conv_general_dilated(
        x, w[:, None, :].astype(x.dtype), window_strides=(1,), padding=[(k - 1, 0)],
        dimension_numbers=("NWC", "WIO", "NWC"), feature_group_count=c)
    return y + b


def _lin_rec_combine(c1, c2):
    a1, b1 = c1
    a2, b2 = c2
    return a1 * a2, a2 * b1 + b2


def rg_lru(x, wr, br, wi, bi, lam):
    bsz, s, w = x.shape
    xh = x.reshape(bsz, s, N_BLOCKS_A, BS_A)
    r = jax.nn.sigmoid(jnp.einsum("bshi,hij->bshj", xh, wr).reshape(bsz, s, w) + br)
    i = jax.nn.sigmoid(jnp.einsum("bshi,hij->bshj", xh, wi).reshape(bsz, s, w) + bi)
    log_a = (LRU_C * r.astype(jnp.float32)) * jax.nn.log_sigmoid(lam.astype(jnp.float32))
    a = jnp.exp(log_a)
    inp = jnp.sqrt(-jnp.expm1(2.0 * log_a)) * (i * x).astype(jnp.float32)
    _, h = lax.associative_scan(_lin_rec_combine, (a, inp), axis=1)
    return h.astype(x.dtype)


def forgetting_attention(q, k, v, f_logit):
    bsz, s, _ = q.shape
    def heads(t):
        return t.reshape(bsz, s, N_HEADS_C, HEAD_DIM_C).transpose(0, 2, 1, 3)
    qh, kh, vh = heads(q), heads(k), heads(v)
    cum = jnp.cumsum(jax.nn.log_sigmoid(f_logit.astype(jnp.float32)), axis=1).transpose(0, 2, 1)
    scale = 1.0 / math.sqrt(HEAD_DIM_C)
    outs = []
    for blk in range(s // Q_BLOCK):
        q0, q1 = blk * Q_BLOCK, (blk + 1) * Q_BLOCK
        logits = jnp.einsum("bhqd,bhkd->bhqk", qh[:, :, q0:q1], kh[:, :, :q1]).astype(jnp.float32) * scale
        logits = logits + cum[:, :, q0:q1, None] - cum[:, :, None, :q1]
        mask = (q0 + jnp.arange(Q_BLOCK))[:, None] >= jnp.arange(q1)[None, :]
        logits = jnp.where(mask, logits, jnp.float32(-1e30))
        p = jax.nn.softmax(logits, axis=-1).astype(vh.dtype)
        outs.append(jnp.einsum("bhqk,bhkd->bhqd", p, vh[:, :, :q1]))
    o = jnp.concatenate(outs, axis=2)
    return o.transpose(0, 2, 1, 3).reshape(bsz, s, W_C)


def hybrid_layer(x, w_in, b_in, conv_a_w, conv_a_b, lru_wr, lru_br, lru_wi, lru_bi, lru_lambda,
                 w_out_a, conv_b_w, conv_b_b, lnb_g, lnb_b, w_out_b, w_out_c, w_o, b_o, ln_g, ln_b):
    bsz, s, d = x.shape
    proj = jnp.einsum("bsd,dn->bsn", x, w_in) + b_in
    xa, ga, ub, gb, q, k, v, f_logit, gc, gm = jnp.split(proj, SPLIT_IDX, axis=-1)

    xa = causal_dwconv(xa, conv_a_w, conv_a_b)
    ya = jnp.einsum("bsw,wd->bsd", rg_lru(xa, lru_wr, lru_br, lru_wi, lru_bi, lru_lambda) * jax.nn.silu(ga), w_out_a)

    u_val, u_gate = jnp.split(ub, 2, axis=-1)
    u = causal_dwconv(u_val * jax.nn.sigmoid(u_gate), conv_b_w, conv_b_b)
    u = jax.nn.silu(layer_norm(u, lnb_g, lnb_b))
    yb = jnp.einsum("bsw,wd->bsd", u * jax.nn.silu(gb), w_out_b)

    yc = jnp.einsum("bsw,wd->bsd", forgetting_attention(q, k, v, f_logit) * jax.nn.silu(gc), w_out_c)

    g = jax.nn.sigmoid(gm).reshape(bsz, s, N_BRANCH, d)
    mixed = g[:, :, 0] * ya + g[:, :, 1] * yb + g[:, :, 2] * yc
    out = jnp.einsum("bsd,de->bse", mixed, w_o) + b_o
    return layer_norm(DEEPNORM_ALPHA * x + out, ln_g, ln_b)


def setup_inputs(seed: int = 0) -> dict:
    key = jax.random.key(seed)
    ks = jax.random.split(key, 24)
    f32 = jnp.float32
    L, D = DEPTH, D_MODEL
    nrm = lambda k, shape, s: jax.random.normal(k, shape, f32) * s
    x = jax.random.normal(ks[0], (BATCH, SEQ, D), f32)
    w_in = nrm(ks[1], (L, D, N_IN), D ** -0.5)
    b_in = nrm(ks[2], (L, N_IN), 0.02)
    b_in = b_in.at[:, FORGET_OFFSET:FORGET_OFFSET + N_HEADS_C].set(
        jax.random.uniform(ks[3], (L, N_HEADS_C), f32, 1.0, 4.0))
    conv_a_w = nrm(ks[4], (L, CONV_A, W_A), CONV_A ** -0.5)
    conv_a_b = nrm(ks[5], (L, W_A), 0.02)
    lru_wr = nrm(ks[6], (L, N_BLOCKS_A, BS_A, BS_A), BS_A ** -0.5)
    lru_br = nrm(ks[7], (L, W_A), 0.02)
    lru_wi = nrm(ks[8], (L, N_BLOCKS_A, BS_A, BS_A), BS_A ** -0.5)
    lru_bi = nrm(ks[9], (L, W_A), 0.02)
    a0 = jax.random.uniform(ks[10], (L, W_A), f32, 0.9, 0.999)
    s0 = a0 ** (1.0 / LRU_C)
    lru_lambda = jnp.log(s0) - jnp.log1p(-s0)
    w_out_a = nrm(ks[11], (L, W_A, D), W_A ** -0.5 * DEEPNORM_BETA)
    conv_b_w = nrm(ks[12], (L, CONV_B, W_B), CONV_B ** -0.5)
    conv_b_b = nrm(ks[13], (L, W_B), 0.02)
    lnb_g = 1.0 + nrm(ks[14], (L, W_B), 0.02)
    lnb_b = nrm(ks[15], (L, W_B), 0.02)
    w_out_b = nrm(ks[16], (L, W_B, D), W_B ** -0.5 * DEEPNORM_BETA)
    w_out_c = nrm(ks[17], (L, W_C, D), W_C ** -0.5 * DEEPNORM_BETA)
    w_o = nrm(ks[18], (L, D, D), D ** -0.5 * DEEPNORM_BETA)
    b_o = nrm(ks[19], (L, D), 0.02)
    ln_g = 1.0 + nrm(ks[20], (L, D), 0.02)
    ln_b = nrm(ks[21], (L, D), 0.02)
    return {"x": x, "w_in": w_in, "b_in": b_in, "conv_a_w": conv_a_w, "conv_a_b": conv_a_b,
            "lru_wr": lru_wr, "lru_br": lru_br, "lru_wi": lru_wi, "lru_bi": lru_bi,
            "lru_lambda": lru_lambda, "w_out_a": w_out_a, "conv_b_w": conv_b_w, "conv_b_b": conv_b_b,
            "lnb_g": lnb_g, "lnb_b": lnb_b, "w_out_b": w_out_b, "w_out_c": w_out_c,
            "w_o": w_o, "b_o": b_o, "ln_g": ln_g, "ln_b": ln_b}


def reference(x, w_in, b_in, conv_a_w, conv_a_b, lru_wr, lru_br, lru_wi, lru_bi, lru_lambda,
              w_out_a, conv_b_w, conv_b_b, lnb_g, lnb_b, w_out_b, w_out_c, w_o, b_o, ln_g, ln_b):
    h = x
    for l in range(DEPTH):
        h = hybrid_layer(h, w_in[l], b_in[l], conv_a_w[l], conv_a_b[l], lru_wr[l], lru_br[l],
                         lru_wi[l], lru_bi[l], lru_lambda[l], w_out_a[l], conv_b_w[l], conv_b_b[l],
                         lnb_g[l], lnb_b[l], w_out_b[l], w_out_c[l], w_o[l], b_o[l], ln_g[l], ln_b[l])
    return h
```

```python
import functools
import math

import jax
import jax.numpy as jnp
from jax import lax
from jax.experimental import pallas as pl
from jax.experimental.pallas import tpu as pltpu

F32 = jnp.float32
BF16 = jnp.bfloat16

LRU_C = 8.0
LN_EPS = 1e-5
CONV_A = 4
CONV_B = 31
N_HEADS = 16
HEAD_DIM = 64
N_BLOCKS_A = 16
LANES = 128
SUBLANES = 8
MXU_DIM = 256
VMEM_LIMIT = 56 * 1024 * 1024

ROW_TILE = 512
CONV_CHUNK = 16
Q_TILE = 256
MERGE_TILE = 256


def _sigmoid(x):
    return 0.5 * jnp.tanh(0.5 * x) + 0.5


def _silu(x):
    return x * _sigmoid(x)


def _log_sigmoid(x):
    return jnp.minimum(x, 0.0) - jnp.log1p(jnp.exp(-jnp.abs(x)))


def _layer_norm(x, g, b):
    mu = jnp.mean(x, axis=-1, keepdims=True)
    xc = x - mu
    var = jnp.mean(xc * xc, axis=-1, keepdims=True)
    return xc * lax.rsqrt(var + LN_EPS) * g + b


def _dot(a, b):
    return jnp.dot(a, b, preferred_element_type=F32)


def _branch_a_kernel(x_ref, w_ref, b_ref, cw_ref, cb_ref, wg_ref, br_ref, bi_ref, lam_ref,
                     wo_ref, o_ref, xbuf, a_buf, h_buf, hstate, *, rows, batch):
    hist = (CONV_A - 1) * batch
    width = cw_ref.shape[1]

    @pl.when(pl.program_id(0) == 0)
    def _():
        xbuf[0:hist, :] = jnp.zeros((hist, width), F32)
        hstate[...] = jnp.zeros_like(hstate)

    p = _dot(x_ref[...].astype(BF16), w_ref[...]) + b_ref[...]
    ga = p[:, width:]
    xbuf[hist:hist + rows, :] = p[:, :width]

    y = cb_ref[...] + cw_ref[0:1, :] * xbuf[0:rows, :]
    for j in range(1, CONV_A):
        y = y + cw_ref[j:j + 1, :] * xbuf[j * batch:j * batch + rows, :]
    xbuf[0:hist, :] = xbuf[rows:rows + hist, :]

    yb = y.astype(BF16)
    r_parts, i_parts = [], []
    for g in range(width // MXU_DIM):
        gates = _dot(yb[:, g * MXU_DIM:(g + 1) * MXU_DIM], wg_ref[g])
        r_parts.append(gates[:, :MXU_DIM])
        i_parts.append(gates[:, MXU_DIM:])
    r = _sigmoid(jnp.concatenate(r_parts, axis=-1) + br_ref[...])
    ig = _sigmoid(jnp.concatenate(i_parts, axis=-1) + bi_ref[...])

    log_a = (LRU_C * r) * _log_sigmoid(lam_ref[...])
    a = jnp.exp(log_a)
    gain = jnp.sqrt(-jnp.tanh(log_a) * (1.0 + a * a))
    a_buf[...] = a
    h_buf[...] = gain * (ig * y)

    def step(t, h):
        r0 = pl.multiple_of(t * batch, batch)
        h = a_buf[pl.ds(r0, batch), :] * h + h_buf[pl.ds(r0, batch), :]
        h_buf[pl.ds(r0, batch), :] = h
        return h

    hstate[...] = lax.fori_loop(0, rows // batch, step, hstate[...], unroll=8)

    ha = h_buf[...] * _silu(ga)
    o_ref[...] = _dot(ha.astype(BF16), wo_ref[...]).astype(o_ref.dtype)


def _branch_a(xt, w, b, cw, cb, wg, br, bi, lam, wo, *, batch):
    n_rows, d = xt.shape
    width = cw.shape[1]
    rows = ROW_TILE
    full = lambda shape: pl.BlockSpec(shape, lambda i: (0,) * len(shape))
    return pl.pallas_call(
        functools.partial(_branch_a_kernel, rows=rows, batch=batch),
        grid=(n_rows // rows,),
        in_specs=[
            pl.BlockSpec((rows, d), lambda i: (i, 0)),
            full(w.shape), full(b.shape), full(cw.shape), full(cb.shape), full(wg.shape),
            full(br.shape), full(bi.shape), full(lam.shape), full(wo.shape),
        ],
        out_specs=pl.BlockSpec((rows, d), lambda i: (i, 0)),
        out_shape=jax.ShapeDtypeStruct((n_rows, d), BF16),
        scratch_shapes=[
            pltpu.VMEM(((CONV_A - 1) * batch + rows, width), F32),
            pltpu.VMEM((rows, width), F32),
            pltpu.VMEM((rows, width), F32),
            pltpu.VMEM((batch, width), F32),
        ],
        compiler_params=pltpu.CompilerParams(
            dimension_semantics=("arbitrary",), vmem_limit_bytes=VMEM_LIMIT),
        name="branch_a",
    )(xt, w, b, cw, cb, wg, br, bi, lam, wo)


def _branch_b_kernel(x_ref, w_ref, b_ref, cw_ref, cb_ref, lng_ref, lnb_ref, wo_ref, o_ref,
                     ubuf, vbuf, *, rows, batch):
    hist = (CONV_B - 1) * batch
    width = cw_ref.shape[1]

    @pl.when(pl.program_id(0) == 0)
    def _():
        ubuf[0:hist, :] = jnp.zeros((hist, width), F32)

    p = _dot(x_ref[...].astype(BF16), w_ref[...]) + b_ref[...]
    gb = p[:, 2 * width:]
    ubuf[hist:hist + rows, :] = p[:, :width] * _sigmoid(p[:, width:2 * width])

    def chunk(c, carry):
        r0 = pl.multiple_of(c * CONV_CHUNK, CONV_CHUNK)
        acc = cb_ref[...] + cw_ref[0:1, :] * ubuf[pl.ds(r0, CONV_CHUNK), :]
        for j in range(1, CONV_B):
            acc = acc + cw_ref[j:j + 1, :] * ubuf[pl.ds(r0 + j * batch, CONV_CHUNK), :]
        vbuf[pl.ds(r0, CONV_CHUNK), :] = acc
        return carry

    lax.fori_loop(0, rows // CONV_CHUNK, chunk, 0)
    ubuf[0:hist, :] = ubuf[rows:rows + hist, :]

    u = _silu(_layer_norm(vbuf[...], lng_ref[...], lnb_ref[...]))
    hb = u * _silu(gb)
    o_ref[...] = _dot(hb.astype(BF16), wo_ref[...]).astype(o_ref.dtype)


def _branch_b(xt, w, b, cw, cb, lng, lnb, wo, *, batch):
    n_rows, d = xt.shape
    width = cw.shape[1]
    rows = ROW_TILE
    assert rows >= (CONV_B - 1) * batch
    full = lambda shape: pl.BlockSpec(shape, lambda i: (0,) * len(shape))
    return pl.pallas_call(
        functools.partial(_branch_b_kernel, rows=rows, batch=batch),
        grid=(n_rows // rows,),
        in_specs=[
            pl.BlockSpec((rows, d), lambda i: (i, 0)),
            full(w.shape), full(b.shape), full(cw.shape), full(cb.shape),
            full(lng.shape), full(lnb.shape), full(wo.shape),
        ],
        out_specs=pl.BlockSpec((rows, d), lambda i: (i, 0)),
        out_shape=jax.ShapeDtypeStruct((n_rows, d), BF16),
        scratch_shapes=[
            pltpu.VMEM(((CONV_B - 1) * batch + rows, width), F32),
            pltpu.VMEM((rows, width), F32),
        ],
        compiler_params=pltpu.CompilerParams(
            dimension_semantics=("arbitrary",), vmem_limit_bytes=VMEM_LIMIT),
        name="branch_b",
    )(xt, w, b, cw, cb, lng, lnb, wo)


def _proj_c_kernel(x_ref, w_ref, b_ref, wf_ref, bf_ref, q_ref, k_ref, v_ref, cum_ref,
                   cstate, *, rows, batch, width):
    @pl.when(pl.program_id(0) == 0)
    def _():
        cstate[...] = jnp.zeros_like(cstate)

    xb = x_ref[...].astype(BF16)
    p = _dot(xb, w_ref[...]) + b_ref[...]
    q_ref[...] = (p[:, :width] * (1.0 / math.sqrt(HEAD_DIM))).astype(q_ref.dtype)
    k_ref[...] = p[:, width:2 * width].astype(k_ref.dtype)
    v_ref[...] = p[:, 2 * width:].astype(v_ref.dtype)

    cum_ref[...] = _log_sigmoid(_dot(xb, wf_ref[...]) + bf_ref[...])

    def step(t, c):
        r0 = pl.multiple_of(t * batch, batch)
        c = c + cum_ref[pl.ds(r0, batch), :]
        cum_ref[pl.ds(r0, batch), :] = c
        return c

    cstate[...] = lax.fori_loop(0, rows // batch, step, cstate[...], unroll=8)


def _proj_c(xt, w, b, wf, bf, *, batch):
    n_rows, d = xt.shape
    width = w.shape[1] // 3
    rows = ROW_TILE
    full = lambda shape: pl.BlockSpec(shape, lambda i: (0,) * len(shape))
    row_spec = lambda cols: pl.BlockSpec((rows, cols), lambda i: (i, 0))
    return pl.pallas_call(
        functools.partial(_proj_c_kernel, rows=rows, batch=batch, width=width),
        grid=(n_rows // rows,),
        in_specs=[row_spec(d), full(w.shape), full(b.shape), full(wf.shape), full(bf.shape)],
        out_specs=[row_spec(width), row_spec(width), row_spec(width), row_spec(LANES)],
        out_shape=[
            jax.ShapeDtypeStruct((n_rows, width), BF16),
            jax.ShapeDtypeStruct((n_rows, width), BF16),
            jax.ShapeDtypeStruct((n_rows, width), BF16),
            jax.ShapeDtypeStruct((n_rows, LANES), F32),
        ],
        scratch_shapes=[pltpu.VMEM((batch, LANES), F32)],
        compiler_params=pltpu.CompilerParams(
            dimension_semantics=("arbitrary",), vmem_limit_bytes=VMEM_LIMIT),
        name="proj_c",
    )(xt, w, b, wf, bf)


def _attn_kernel(q_ref, k_ref, v_ref, cq_ref, ck_ref, o_ref, *, tile):
    qi = pl.program_id(1)
    lo = lax.broadcasted_iota(jnp.int32, (1, LANES), 1) < HEAD_DIM
    row = lax.broadcasted_iota(jnp.int32, (tile, tile), 0)
    col = lax.broadcasted_iota(jnp.int32, (tile, tile), 1)
    causal = row >= col
    nt = (((1,), (1,)), ((), ()))
    zero = jnp.zeros((), BF16)

    for hp in range(N_HEADS // 2):
        lanes = slice(hp * LANES, (hp + 1) * LANES)
        q2 = q_ref[:, lanes]
        q_heads = (jnp.where(lo, q2, zero), jnp.where(lo, zero, q2))
        cq_heads = (cq_ref[:, 2 * hp:2 * hp + 1], cq_ref[:, 2 * hp + 1:2 * hp + 2])

        def kv_block(j, carry, masked):
            m_prev, l_prev, acc = carry
            k0 = pl.multiple_of(j * tile, tile)
            k2 = k_ref[pl.ds(k0, tile), lanes]
            v2 = v_ref[pl.ds(k0, tile), lanes]
            v_heads = (jnp.where(lo, v2, zero), jnp.where(lo, zero, v2))
            m_new, l_new, alphas = [], [], []
            pv = None
            for h in range(2):
                ck = ck_ref[2 * hp + h, pl.ds(j, 1), :]
                z = lax.dot_general(q_heads[h], k2, nt, preferred_element_type=F32)
                z = z + cq_heads[h] - ck
                if masked:
                    z = jnp.where(causal, z, -1e30)
                m = jnp.maximum(m_prev[h], jnp.max(z, axis=-1, keepdims=True))
                alpha = jnp.exp(m_prev[h] - m)
                p = jnp.exp(z - m)
                m_new.append(m)
                l_new.append(alpha * l_prev[h] + jnp.sum(p, axis=-1, keepdims=True))
                alphas.append(alpha)
                d = _dot(p.astype(BF16), v_heads[h])
                pv = d if pv is None else pv + d
            acc = acc * jnp.where(lo, alphas[0], alphas[1]) + pv
            return tuple(m_new), tuple(l_new), acc

        neg = jnp.full((tile, 1), -jnp.inf, F32)
        zcol = jnp.zeros((tile, 1), F32)
        carry = ((neg, neg), (zcol, zcol), jnp.zeros((tile, LANES), F32))
        carry = lax.fori_loop(0, qi, functools.partial(kv_block, masked=False), carry)
        _, l_fin, acc = kv_block(qi, carry, masked=True)
        o_ref[:, lanes] = (acc / jnp.where(lo, l_fin[0], l_fin[1])).astype(o_ref.dtype)


def _attention(q, k, v, cum, cum_t, *, batch):
    n_rows, width = q.shape
    seq = n_rows // batch
    tile = Q_TILE
    q2, k2, v2 = (a.reshape(seq, batch * width) for a in (q, k, v))
    cum2 = cum.reshape(seq, batch * LANES)
    out = pl.pallas_call(
        functools.partial(_attn_kernel, tile=tile),
        grid=(batch, seq // tile),
        in_specs=[
            pl.BlockSpec((tile, width), lambda b, i: (i, b)),
            pl.BlockSpec((seq, width), lambda b, i: (0, b)),
            pl.BlockSpec((seq, width), lambda b, i: (0, b)),
            pl.BlockSpec((tile, LANES), lambda b, i: (i, b)),
            pl.BlockSpec((None, N_HEADS, seq // tile, tile), lambda b, i: (b, 0, 0, 0)),
        ],
        out_specs=pl.BlockSpec((tile, width), lambda b, i: (i, b)),
        out_shape=jax.ShapeDtypeStruct((seq, batch * width), BF16),
        compiler_params=pltpu.CompilerParams(
            dimension_semantics=("arbitrary", "arbitrary"), vmem_limit_bytes=VMEM_LIMIT),
        name="fox_attention",
    )(q2, k2, v2, cum2, cum_t)
    return out.reshape(n_rows, width)


def _merge_kernel(x_ref, ya_ref, yb_ref, oc_ref, wg_ref, bg_ref, woc_ref, wo_ref, bo_ref,
                  lng_ref, lnb_ref, out_ref, *, alpha):
    d = x_ref.shape[1]
    x = x_ref[...]
    pg = _dot(x.astype(BF16), wg_ref[...]) + bg_ref[...]
    hc = oc_ref[...].astype(F32) * _silu(pg[:, :d])
    yc = _dot(hc.astype(BF16), woc_ref[...])
    mixed = (_sigmoid(pg[:, d:2 * d]) * ya_ref[...].astype(F32)
             + _sigmoid(pg[:, 2 * d:3 * d]) * yb_ref[...].astype(F32)
             + _sigmoid(pg[:, 3 * d:]) * yc)
    out = _dot(mixed.astype(BF16), wo_ref[...]) + bo_ref[...]
    res = _layer_norm(alpha * x + out, lng_ref[...], lnb_ref[...])
    out_ref[...] = res.astype(out_ref.dtype)


def _merge(xt, ya, yb, oc, wg, bg, woc, wo, bo, lng, lnb, *, batch, alpha, batch_major_out):
    n_rows, d = xt.shape
    seq = n_rows // batch
    tile = MERGE_TILE
    view = lambda a: a.reshape(seq, batch * d)
    tm_spec = pl.BlockSpec((tile, d), lambda b, i: (i, b))
    full = lambda shape: pl.BlockSpec(shape, lambda b, i: (0,) * len(shape))
    if batch_major_out:
        out_spec = pl.BlockSpec((None, tile, d), lambda b, i: (b, i, 0))
        out_shape = jax.ShapeDtypeStruct((batch, seq, d), F32)
    else:
        out_spec = tm_spec
        out_shape = jax.ShapeDtypeStruct((seq, batch * d), F32)
    out = pl.pallas_call(
        functools.partial(_merge_kernel, alpha=alpha),
        grid=(batch, seq // tile),
        in_specs=[tm_spec, tm_spec, tm_spec, tm_spec,
                  full(wg.shape), full(bg.shape), full(woc.shape), full(wo.shape),
                  full(bo.shape), full(lng.shape), full(lnb.shape)],
        out_specs=out_spec,
        out_shape=out_shape,
        compiler_params=pltpu.CompilerParams(
            dimension_semantics=("arbitrary", "arbitrary"), vmem_limit_bytes=VMEM_LIMIT),
        name="merge_out",
    )(view(xt), view(ya), view(yb), view(oc), wg, bg, woc, wo, bo, lng, lnb)
    return out if batch_major_out else out.reshape(n_rows, d)


def _block_diag_gates(wr, wi):
    per_tile = MXU_DIM // wr.shape[1]
    eye = jnp.eye(per_tile, dtype=wr.dtype)

    def bd(w):
        w4 = w.reshape(-1, per_tile, w.shape[1], w.shape[2])
        return jnp.einsum("gaij,ab->gaibj", w4, eye).reshape(-1, MXU_DIM, MXU_DIM)

    return jnp.concatenate([bd(wr), bd(wi)], axis=-1).astype(BF16)


def kernel(x, w_in, b_in, conv_a_w, conv_a_b, lru_wr, lru_br, lru_wi, lru_bi, lru_lambda,
           w_out_a, conv_b_w, conv_b_b, lnb_g, lnb_b, w_out_b, w_out_c, w_o, b_o, ln_g, ln_b):
    batch, seq, d = x.shape
    depth = w_in.shape[0]
    assert batch == SUBLANES and d == N_HEADS * HEAD_DIM
    alpha = float((2 * depth) ** 0.25)
    row = lambda a: a.reshape(1, -1)

    o_a, o_b, o_q, o_f, o_gc, o_gm = 0, 2 * d, 5 * d, 8 * d, 8 * d + N_HEADS, 9 * d + N_HEADS

    xt = x.transpose(1, 0, 2).reshape(seq * batch, d)
    for l in range(depth):
        w, b = w_in[l], b_in[l]
        wb = lambda lo, hi: w[:, lo:hi].astype(BF16)
        bb = lambda lo, hi: row(b[lo:hi])

        ya = _branch_a(xt, wb(o_a, o_b), bb(o_a, o_b), conv_a_w[l], row(conv_a_b[l]),
                       _block_diag_gates(lru_wr[l], lru_wi[l]), row(lru_br[l]), row(lru_bi[l]),
                       row(lru_lambda[l]), w_out_a[l].astype(BF16), batch=batch)
        yb = _branch_b(xt, wb(o_b, o_q), bb(o_b, o_q), conv_b_w[l], row(conv_b_b[l]),
                       row(lnb_g[l]), row(lnb_b[l]), w_out_b[l].astype(BF16), batch=batch)

        wf = jnp.pad(w[:, o_f:o_gc], ((0, 0), (0, LANES - N_HEADS))).astype(BF16)
        bf = jnp.pad(b[o_f:o_gc], (0, LANES - N_HEADS)).reshape(1, LANES)
        q, k, v, cum = _proj_c(xt, wb(o_q, o_f), bb(o_q, o_f), wf, bf, batch=batch)
        cum_t = (cum.reshape(seq, batch, LANES)[:, :, :N_HEADS].transpose(1, 2, 0)
                 .reshape(batch, N_HEADS, seq // Q_TILE, Q_TILE))
        oc = _attention(q, k, v, cum, cum_t, batch=batch)

        last = l == depth - 1
        xt = _merge(xt, ya, yb, oc, wb(o_gc, w.shape[1]), bb(o_gc, w.shape[1]),
                    w_out_c[l].astype(BF16), w_o[l].astype(BF16), row(b_o[l]),
                    row(ln_g[l]), row(ln_b[l]), batch=batch, alpha=alpha, batch_major_out=last)
    return xt
```

```python
import functools
import math

import jax
import jax.numpy as jnp
from jax import lax
from jax.experimental import pallas as pl
from jax.experimental.pallas import tpu as pltpu

F32 = jnp.float32
BF16 = jnp.bfloat16

LRU_C = 8.0
LN_EPS = 1e-5
CONV_A = 4
CONV_B = 31
N_HEADS = 16
HEAD_DIM = 64
LANES = 128
SUBLANES = 8
MXU_DIM = 256
VMEM_LIMIT = 56 * 1024 * 1024
LOG2E = math.log2(math.e)
MASKED = -1e30

T_TILE = 64
CONV_CHUNK = 64
Q_TILE = 256
STRIP = 16
MERGE_TILE = 256


def _sigmoid(x):
    return 0.5 * jnp.tanh(0.5 * x) + 0.5


def _silu(x):
    return x * _sigmoid(x)


def _log_sigmoid(x):
    return jnp.minimum(x, 0.0) - jnp.log1p(jnp.exp(-jnp.abs(x)))


def _layer_norm(x, g, b):
    mu = jnp.mean(x, axis=-1, keepdims=True)
    xc = x - mu
    var = jnp.mean(xc * xc, axis=-1, keepdims=True)
    return xc * lax.rsqrt(var + LN_EPS) * g + b


def _dot(a, b):
    return jnp.dot(a, b, preferred_element_type=F32)


def _lane_block(c):
    return slice(c * LANES, (c + 1) * LANES)


def _store_time_major(dst, row0, val, batch, t):
    for c in range(dst.shape[0]):
        for b in range(batch):
            dst[c, pl.ds(row0 + b, t, stride=batch), :] = val[b * t:(b + 1) * t, _lane_block(c)]


def _load_batch_major(src, batch, t):
    return jnp.concatenate(
        [jnp.concatenate([src[c, pl.ds(b, t, stride=batch), :] for c in range(src.shape[0])], axis=-1)
         for b in range(batch)], axis=0)


def _full_spec(shape):
    return pl.BlockSpec(shape, lambda *_: (0,) * len(shape))


def _seq_params():
    return pltpu.CompilerParams(dimension_semantics=("arbitrary",), vmem_limit_bytes=VMEM_LIMIT)


def _branch_a_kernel(x_ref, w_ref, b_ref, cw_ref, cb_ref, wg_ref, br_ref, bi_ref, lam_ref,
                     wo_ref, o_ref, xbuf, a_buf, h_buf, hstate, *, t, batch):
    rows = t * batch
    hist = (CONV_A - 1) * batch
    slabs = xbuf.shape[0]
    width = slabs * LANES

    @pl.when(pl.program_id(0) == 0)
    def _():
        xbuf[:, 0:hist, :] = jnp.zeros((slabs, hist, LANES), F32)
        hstate[...] = jnp.zeros_like(hstate)

    xb = x_ref[...].reshape(rows, x_ref.shape[2]).astype(BF16)
    p = _dot(xb, w_ref[...]) + b_ref[...]
    ga = p[:, width:]
    _store_time_major(xbuf, hist, p[:, :width], batch, t)

    ys = []
    for c in range(slabs):
        lanes = _lane_block(c)
        y = cb_ref[:, lanes] + cw_ref[0:1, lanes] * xbuf[c, 0:rows, :]
        for j in range(1, CONV_A):
            y = y + cw_ref[j:j + 1, lanes] * xbuf[c, j * batch:j * batch + rows, :]
        xbuf[c, 0:hist, :] = xbuf[c, rows:rows + hist, :]
        ys.append(y)

    log_sig = _log_sigmoid(lam_ref[...])
    per_tile = MXU_DIM // LANES
    for g in range(width // MXU_DIM):
        yb = jnp.concatenate(ys[g * per_tile:(g + 1) * per_tile], axis=-1).astype(BF16)
        gates = _dot(yb, wg_ref[g])
        for s in range(per_tile):
            c = g * per_tile + s
            lanes = _lane_block(c)
            r = _sigmoid(gates[:, _lane_block(s)] + br_ref[:, lanes])
            ig = _sigmoid(gates[:, _lane_block(per_tile + s)] + bi_ref[:, lanes])
            log_a = (LRU_C * r) * log_sig[:, lanes]
            a = jnp.exp(log_a)
            gain = jnp.sqrt(-jnp.tanh(log_a) * (1.0 + a * a))
            a_buf[c] = a
            h_buf[c] = gain * (ig * ys[c])

    def step(tt, hs):
        r0 = pl.multiple_of(tt * batch, batch)
        out = []
        for c in range(slabs):
            h = a_buf[c, pl.ds(r0, batch), :] * hs[c] + h_buf[c, pl.ds(r0, batch), :]
            h_buf[c, pl.ds(r0, batch), :] = h
            out.append(h)
        return tuple(out)

    hs = lax.fori_loop(0, t, step, tuple(hstate[c] for c in range(slabs)), unroll=8)
    for c in range(slabs):
        hstate[c] = hs[c]

    ha = _load_batch_major(h_buf, batch, t) * _silu(ga)
    o_ref[...] = _dot(ha.astype(BF16), wo_ref[...]).astype(o_ref.dtype).reshape(o_ref.shape)


def _branch_a(x, w, b, cw, cb, wg, br, bi, lam, wo):
    batch, seq, d = x.shape
    width = cw.shape[1]
    slabs = width // LANES
    t = T_TILE
    rows = t * batch
    return pl.pallas_call(
        functools.partial(_branch_a_kernel, t=t, batch=batch),
        grid=(seq // t,),
        in_specs=[pl.BlockSpec((batch, t, d), lambda i: (0, i, 0))]
        + [_full_spec(a.shape) for a in (w, b, cw, cb, wg, br, bi, lam, wo)],
        out_specs=pl.BlockSpec((batch, t, d), lambda i: (0, i, 0)),
        out_shape=jax.ShapeDtypeStruct((batch, seq, d), BF16),
        scratch_shapes=[
            pltpu.VMEM((slabs, (CONV_A - 1) * batch + rows, LANES), F32),
            pltpu.VMEM((slabs, rows, LANES), F32),
            pltpu.VMEM((slabs, rows, LANES), F32),
            pltpu.VMEM((slabs, batch, LANES), F32),
        ],
        compiler_params=_seq_params(),
        name="branch_a",
    )(x, w, b, cw, cb, wg, br, bi, lam, wo)


def _branch_b_kernel(x_ref, w_ref, b_ref, cw_ref, cb_ref, lng_ref, lnb_ref, wo_ref, o_ref,
                     ubuf, vbuf, wrep, *, t, batch):
    rows = t * batch
    hist = (CONV_B - 1) * batch
    slabs = ubuf.shape[0]
    width = slabs * LANES
    reps = CONV_CHUNK // SUBLANES

    @pl.when(pl.program_id(0) == 0)
    def _():
        ubuf[:, 0:hist, :] = jnp.zeros((slabs, hist, LANES), F32)
        for j in range(CONV_B):
            wrep[j] = jnp.broadcast_to(cw_ref[j:j + 1, :], (SUBLANES, width))
        wrep[CONV_B] = jnp.broadcast_to(cb_ref[...], (SUBLANES, width))

    xb = x_ref[...].reshape(rows, x_ref.shape[2]).astype(BF16)
    p = _dot(xb, w_ref[...]) + b_ref[...]
    gb = p[:, 2 * width:]
    _store_time_major(ubuf, hist, p[:, :width] * _sigmoid(p[:, width:2 * width]), batch, t)

    def tile_rows(v):
        return jnp.concatenate([v] * reps, axis=0)

    def chunk(ci, carry):
        r0 = pl.multiple_of(ci * CONV_CHUNK, CONV_CHUNK)
        for c in range(slabs):
            lanes = _lane_block(c)
            acc = tile_rows(wrep[CONV_B, :, lanes])
            for j in range(CONV_B):
                acc = acc + tile_rows(wrep[j, :, lanes]) * ubuf[c, pl.ds(r0 + j * batch, CONV_CHUNK), :]
            vbuf[c, pl.ds(r0, CONV_CHUNK), :] = acc
        return carry

    lax.fori_loop(0, rows // CONV_CHUNK, chunk, 0)
    for c in range(slabs):
        ubuf[c, 0:hist, :] = ubuf[c, rows:rows + hist, :]

    u = _silu(_layer_norm(_load_batch_major(vbuf, batch, t), lng_ref[...], lnb_ref[...]))
    hb = u * _silu(gb)
    o_ref[...] = _dot(hb.astype(BF16), wo_ref[...]).astype(o_ref.dtype).reshape(o_ref.shape)


def _branch_b(x, w, b, cw, cb, lng, lnb, wo):
    batch, seq, d = x.shape
    width = cw.shape[1]
    slabs = width // LANES
    t = T_TILE
    rows = t * batch
    assert rows >= (CONV_B - 1) * batch and rows % CONV_CHUNK == 0
    return pl.pallas_call(
        functools.partial(_branch_b_kernel, t=t, batch=batch),
        grid=(seq // t,),
        in_specs=[pl.BlockSpec((batch, t, d), lambda i: (0, i, 0))]
        + [_full_spec(a.shape) for a in (w, b, cw, cb, lng, lnb, wo)],
        out_specs=pl.BlockSpec((batch, t, d), lambda i: (0, i, 0)),
        out_shape=jax.ShapeDtypeStruct((batch, seq, d), BF16),
        scratch_shapes=[
            pltpu.VMEM((slabs, (CONV_B - 1) * batch + rows, LANES), F32),
            pltpu.VMEM((slabs, rows, LANES), F32),
            pltpu.VMEM((CONV_B + 1, SUBLANES, width), F32),
        ],
        compiler_params=_seq_params(),
        name="branch_b",
    )(x, w, b, cw, cb, lng, lnb, wo)


def _proj_c_kernel(x_ref, w_ref, b_ref, wf_ref, bf_ref, q_ref, k_ref, v_ref, cum_ref,
                   cbuf, cstate, *, t, batch, width):
    rows = t * batch

    @pl.when(pl.program_id(0) == 0)
    def _():
        cstate[...] = jnp.zeros_like(cstate)

    xb = x_ref[...].reshape(rows, x_ref.shape[2]).astype(BF16)
    p = _dot(xb, w_ref[...]) + b_ref[...]
    q_scale = LOG2E / math.sqrt(HEAD_DIM)
    q_ref[...] = (p[:, :width] * q_scale).astype(q_ref.dtype).reshape(q_ref.shape)
    k_ref[...] = p[:, width:2 * width].astype(k_ref.dtype).reshape(k_ref.shape)
    v_ref[...] = p[:, 2 * width:].astype(v_ref.dtype).reshape(v_ref.shape)

    lf = _log_sigmoid(_dot(xb, wf_ref[...]) + bf_ref[...]) * LOG2E
    _store_time_major(cbuf, 0, lf, batch, t)

    def step(tt, c):
        r0 = pl.multiple_of(tt * batch, batch)
        c = c + cbuf[0, pl.ds(r0, batch), :]
        cbuf[0, pl.ds(r0, batch), :] = c
        return c

    cstate[...] = lax.fori_loop(0, t, step, cstate[...], unroll=8)
    cum_ref[...] = _load_batch_major(cbuf, batch, t).reshape(cum_ref.shape)


def _proj_c(x, w, b, wf, bf):
    batch, seq, d = x.shape
    width = w.shape[1] // 3
    t = T_TILE
    spec = lambda cols: pl.BlockSpec((batch, t, cols), lambda i: (0, i, 0))
    return pl.pallas_call(
        functools.partial(_proj_c_kernel, t=t, batch=batch, width=width),
        grid=(seq // t,),
        in_specs=[spec(d)] + [_full_spec(a.shape) for a in (w, b, wf, bf)],
        out_specs=[spec(width), spec(width), spec(width), spec(LANES)],
        out_shape=[
            jax.ShapeDtypeStruct((batch, seq, width), BF16),
            jax.ShapeDtypeStruct((batch, seq, width), BF16),
            jax.ShapeDtypeStruct((batch, seq, width), BF16),
            jax.ShapeDtypeStruct((batch, seq, LANES), F32),
        ],
        scratch_shapes=[pltpu.VMEM((1, t * batch, LANES), F32), pltpu.VMEM((batch, LANES), F32)],
        compiler_params=_seq_params(),
        name="proj_c",
    )(x, w, b, wf, bf)


def _attn_kernel(q_ref, k_ref, v_ref, cq_ref, ck_ref, o_ref, s_ref, p_ref, vm_ref, ckb_ref, l_ref,
                 *, tile):
    hp = pl.program_id(1)
    seq = q_ref.shape[0]
    half = tile // 2
    lane = lax.broadcasted_iota(jnp.int32, (1, LANES), 1)
    head_lanes = (lane < HEAD_DIM, lane >= HEAD_DIM)
    zero = jnp.zeros((), BF16)
    nt = (((1,), (1,)), ((), ()))
    srow = lax.broadcasted_iota(jnp.int32, (STRIP, LANES), 0)
    scol = lax.broadcasted_iota(jnp.int32, (STRIP, LANES), 1)

    v2 = v_ref[...]
    for h in range(2):
        vm_ref[h] = jnp.where(head_lanes[h], v2, zero)
        ckb_ref[h] = jnp.broadcast_to(ck_ref[pl.ds(2 * hp + h, 1), :], (SUBLANES, seq))

    for i in range(seq // tile):
        q0 = i * tile
        klen = q0 + tile
        q2 = q_ref[q0:klen, :]
        cq2 = cq_ref[q0:klen, :]
        outs = []
        for h in range(2):
            qh = jnp.where(head_lanes[h], q2, zero)
            s_ref[h, :, 0:klen] = lax.dot_general(qh, k_ref[0:klen, :], nt, preferred_element_type=F32)
            cq = jnp.sum(jnp.where(lane == 2 * hp + h, cq2, 0.0), axis=-1, keepdims=True)
            p_ref[h, 0:half, klen - half:klen] = jnp.zeros((half, half), BF16)
            for r in range(tile // STRIP):
                r0 = r * STRIP
                kh = q0 + half * (r0 // half + 1)
                ck = jnp.concatenate([ckb_ref[h, :, 0:kh]] * (STRIP // SUBLANES), axis=0)
                z = s_ref[h, r0:r0 + STRIP, 0:kh] - ck
                keep = scol <= srow + (r0 % half)
                diag = jnp.where(keep, z[:, kh - half:], MASKED)
                z = diag if kh == half else jnp.concatenate([z[:, :kh - half], diag], axis=-1)
                c = cq[r0:r0 + STRIP]
                m = jnp.max(z, axis=-1, keepdims=True) + c
                pz = jnp.exp2(z + (c - m))
                l_ref[h, r0:r0 + STRIP, :] = jnp.sum(pz, axis=-1, keepdims=True)
                p_ref[h, r0:r0 + STRIP, 0:kh] = pz.astype(BF16)
            pv = _dot(p_ref[h, :, 0:klen], vm_ref[h, 0:klen, :])
            outs.append(pv / l_ref[h])
        o_ref[q0:klen, :] = (outs[0] + outs[1]).astype(o_ref.dtype)


def _attention(q, k, v, cum, cum_t):
    batch, seq, width = q.shape
    tile = Q_TILE
    assert tile // 2 == LANES and tile % STRIP == 0
    pair = pl.BlockSpec((None, seq, LANES), lambda b, hp: (b, 0, hp))
    return pl.pallas_call(
        functools.partial(_attn_kernel, tile=tile),
        grid=(batch, width // LANES),
        in_specs=[pair, pair, pair,
                  pl.BlockSpec((None, seq, LANES), lambda b, hp: (b, 0, 0)),
                  pl.BlockSpec((None, N_HEADS, seq), lambda b, hp: (b, 0, 0))],
        out_specs=pair,
        out_shape=jax.ShapeDtypeStruct((batch, seq, width), BF16),
        scratch_shapes=[
            pltpu.VMEM((2, tile, seq), F32),
            pltpu.VMEM((2, tile, seq), BF16),
            pltpu.VMEM((2, seq, LANES), BF16),
            pltpu.VMEM((2, SUBLANES, seq), F32),
            pltpu.VMEM((2, tile, 1), F32),
        ],
        compiler_params=pltpu.CompilerParams(
            dimension_semantics=("arbitrary", "arbitrary"), vmem_limit_bytes=VMEM_LIMIT),
        name="fox_attention",
    )(q, k, v, cum, cum_t)


def _merge_kernel(x_ref, ya_ref, yb_ref, oc_ref, wg_ref, bg_ref, woc_ref, wo_ref, bo_ref,
                  lng_ref, lnb_ref, out_ref, *, alpha):
    d = x_ref.shape[1]
    x = x_ref[...]
    pg = _dot(x.astype(BF16), wg_ref[...]) + bg_ref[...]
    hc = oc_ref[...].astype(F32) * _silu(pg[:, :d])
    yc = _dot(hc.astype(BF16), woc_ref[...])
    mixed = (_sigmoid(pg[:, d:2 * d]) * ya_ref[...].astype(F32)
             + _sigmoid(pg[:, 2 * d:3 * d]) * yb_ref[...].astype(F32)
             + _sigmoid(pg[:, 3 * d:]) * yc)
    out = _dot(mixed.astype(BF16), wo_ref[...]) + bo_ref[...]
    out_ref[...] = _layer_norm(alpha * x + out, lng_ref[...], lnb_ref[...]).astype(out_ref.dtype)


def _merge(x, ya, yb, oc, wg, bg, woc, wo, bo, lng, lnb, *, alpha):
    batch, seq, d = x.shape
    tile = MERGE_TILE
    rows = pl.BlockSpec((None, tile, d), lambda b, i: (b, i, 0))
    return pl.pallas_call(
        functools.partial(_merge_kernel, alpha=alpha),
        grid=(batch, seq // tile),
        in_specs=[rows, rows, rows, rows]
        + [_full_spec(a.shape) for a in (wg, bg, woc, wo, bo, lng, lnb)],
        out_specs=rows,
        out_shape=jax.ShapeDtypeStruct((batch, seq, d), F32),
        compiler_params=pltpu.CompilerParams(
            dimension_semantics=("arbitrary", "arbitrary"), vmem_limit_bytes=VMEM_LIMIT),
        name="merge_out",
    )(x, ya, yb, oc, wg, bg, woc, wo, bo, lng, lnb)


def _block_diag_gates(wr, wi):
    per_tile = MXU_DIM // wr.shape[1]
    eye = jnp.eye(per_tile, dtype=wr.dtype)

    def bd(w):
        w4 = w.reshape(-1, per_tile, w.shape[1], w.shape[2])
        return jnp.einsum("gaij,ab->gaibj", w4, eye).reshape(-1, MXU_DIM, MXU_DIM)

    return jnp.concatenate([bd(wr), bd(wi)], axis=-1).astype(BF16)


def kernel(x, w_in, b_in, conv_a_w, conv_a_b, lru_wr, lru_br, lru_wi, lru_bi, lru_lambda,
           w_out_a, conv_b_w, conv_b_b, lnb_g, lnb_b, w_out_b, w_out_c, w_o, b_o, ln_g, ln_b):
    batch, seq, d = x.shape
    depth = w_in.shape[0]
    assert batch == SUBLANES and d == N_HEADS * HEAD_DIM
    alpha = float((2 * depth) ** 0.25)
    row = lambda a: a.reshape(1, -1)

    o_a, o_b, o_q, o_f, o_gc = 0, 2 * d, 5 * d, 8 * d, 8 * d + N_HEADS

    for l in range(depth):
        w, b = w_in[l], b_in[l]
        wb = lambda lo, hi: w[:, lo:hi].astype(BF16)
        bb = lambda lo, hi: row(b[lo:hi])

        ya = _branch_a(x, wb(o_a, o_b), bb(o_a, o_b), conv_a_w[l], row(conv_a_b[l]),
                       _block_diag_gates(lru_wr[l], lru_wi[l]), row(lru_br[l]), row(lru_bi[l]),
                       row(lru_lambda[l]), w_out_a[l].astype(BF16))
        yb = _branch_b(x, wb(o_b, o_q), bb(o_b, o_q), conv_b_w[l], row(conv_b_b[l]),
                       row(lnb_g[l]), row(lnb_b[l]), w_out_b[l].astype(BF16))

        wf = jnp.pad(w[:, o_f:o_gc], ((0, 0), (0, LANES - N_HEADS))).astype(BF16)
        bf = jnp.pad(b[o_f:o_gc], (0, LANES - N_HEADS)).reshape(1, LANES)
        q, k, v, cum = _proj_c(x, wb(o_q, o_f), bb(o_q, o_f), wf, bf)
        cum_t = cum[:, :, :N_HEADS].transpose(0, 2, 1)
        oc = _attention(q, k, v, cum, cum_t)

        x = _merge(x, ya, yb, oc, wb(o_gc, w.shape[1]), bb(o_gc, w.shape[1]),
                   w_out_c[l].astype(BF16), w_o[l].astype(BF16), row(b_o[l]),
                   row(ln_g[l]), row(ln_b[l]), alpha=alpha)
    return x
```

```python
import functools
import math

import jax
import jax.numpy as jnp
import numpy as np
from jax import lax
from jax.experimental import pallas as pl
from jax.experimental.pallas import tpu as pltpu

F32 = jnp.float32
BF16 = jnp.bfloat16

LRU_C = 8.0
LN_EPS = 1e-5
CONV_A = 4
CONV_B = 31
N_HEADS = 16
HEAD_DIM = 64
LANES = 128
SUBLANES = 8
MXU_DIM = 256
VMEM_LIMIT = 56 * 1024 * 1024
LOG2E = math.log2(math.e)
MASKED = -1e30
SPLIT = 3
QUERY_LANES = 64

T_TILE = 64
CONV_CHUNK = 64
Q_TILE = 256
KEY_CHUNK = 128
AHEAD = 3
MERGE_TILE = 512


def _sigmoid(x):
    return 0.5 * jnp.tanh(0.5 * x) + 0.5


def _silu(x):
    return x * _sigmoid(x)


def _log_sigmoid(x):
    return jnp.minimum(x, 0.0) - jnp.log1p(jnp.exp(-jnp.abs(x)))


def _layer_norm(x, g, b):
    mu = jnp.mean(x, axis=-1, keepdims=True)
    xc = x - mu
    var = jnp.mean(xc * xc, axis=-1, keepdims=True)
    return xc * lax.rsqrt(var + LN_EPS) * g + b


def _dot(a, b):
    return jnp.dot(a, b, preferred_element_type=F32)


def _lane_block(c):
    return slice(c * LANES, (c + 1) * LANES)


def _store_time_major(dst, row0, val, batch, t):
    for c in range(dst.shape[0]):
        for b in range(batch):
            dst[c, pl.ds(row0 + b, t, stride=batch), :] = val[b * t:(b + 1) * t, _lane_block(c)]


def _load_batch_major(src, batch, t):
    return jnp.concatenate(
        [jnp.concatenate([src[c, pl.ds(b, t, stride=batch), :] for c in range(src.shape[0])], axis=-1)
         for b in range(batch)], axis=0)


def _full_spec(shape):
    return pl.BlockSpec(shape, lambda *_: (0,) * len(shape))


def _seq_params():
    return pltpu.CompilerParams(dimension_semantics=("arbitrary",), vmem_limit_bytes=VMEM_LIMIT)


def _branch_a_kernel(x_ref, w_ref, b_ref, cw_ref, cb_ref, wg_ref, br_ref, bi_ref, lam_ref,
                     wo_ref, o_ref, xbuf, a_buf, h_buf, hstate, *, t, batch):
    rows = t * batch
    hist = (CONV_A - 1) * batch
    slabs = xbuf.shape[0]
    width = slabs * LANES

    @pl.when(pl.program_id(0) == 0)
    def _():
        xbuf[:, 0:hist, :] = jnp.zeros((slabs, hist, LANES), F32)
        hstate[...] = jnp.zeros_like(hstate)

    xb = x_ref[...].reshape(rows, x_ref.shape[2]).astype(BF16)
    p = _dot(xb, w_ref[...]) + b_ref[...]
    ga = p[:, width:]
    _store_time_major(xbuf, hist, p[:, :width], batch, t)

    ys = []
    for c in range(slabs):
        lanes = _lane_block(c)
        y = cb_ref[:, lanes] + cw_ref[0:1, lanes] * xbuf[c, 0:rows, :]
        for j in range(1, CONV_A):
            y = y + cw_ref[j:j + 1, lanes] * xbuf[c, j * batch:j * batch + rows, :]
        xbuf[c, 0:hist, :] = xbuf[c, rows:rows + hist, :]
        ys.append(y)

    log_sig = _log_sigmoid(lam_ref[...])
    per_tile = MXU_DIM // LANES
    for g in range(width // MXU_DIM):
        yb = jnp.concatenate(ys[g * per_tile:(g + 1) * per_tile], axis=-1).astype(BF16)
        gates = _dot(yb, wg_ref[g])
        for s in range(per_tile):
            c = g * per_tile + s
            lanes = _lane_block(c)
            r = _sigmoid(gates[:, _lane_block(s)] + br_ref[:, lanes])
            ig = _sigmoid(gates[:, _lane_block(per_tile + s)] + bi_ref[:, lanes])
            log_a = (LRU_C * r) * log_sig[:, lanes]
            a = jnp.exp(log_a)
            gain = jnp.sqrt(-jnp.tanh(log_a) * (1.0 + a * a))
            a_buf[c] = a
            h_buf[c] = gain * (ig * ys[c])

    def step(tt, hs):
        r0 = pl.multiple_of(tt * batch, batch)
        out = []
        for c in range(slabs):
            h = a_buf[c, pl.ds(r0, batch), :] * hs[c] + h_buf[c, pl.ds(r0, batch), :]
            h_buf[c, pl.ds(r0, batch), :] = h
            out.append(h)
        return tuple(out)

    hs = lax.fori_loop(0, t, step, tuple(hstate[c] for c in range(slabs)), unroll=8)
    for c in range(slabs):
        hstate[c] = hs[c]

    ha = _load_batch_major(h_buf, batch, t) * _silu(ga)
    o_ref[...] = _dot(ha.astype(BF16), wo_ref[...]).astype(o_ref.dtype).reshape(o_ref.shape)


def _branch_a(x, w, b, cw, cb, wg, br, bi, lam, wo):
    batch, seq, d = x.shape
    width = cw.shape[1]
    slabs = width // LANES
    t = T_TILE
    rows = t * batch
    return pl.pallas_call(
        functools.partial(_branch_a_kernel, t=t, batch=batch),
        grid=(seq // t,),
        in_specs=[pl.BlockSpec((batch, t, d), lambda i: (0, i, 0))]
        + [_full_spec(a.shape) for a in (w, b, cw, cb, wg, br, bi, lam, wo)],
        out_specs=pl.BlockSpec((batch, t, d), lambda i: (0, i, 0)),
        out_shape=jax.ShapeDtypeStruct((batch, seq, d), BF16),
        scratch_shapes=[
            pltpu.VMEM((slabs, (CONV_A - 1) * batch + rows, LANES), F32),
            pltpu.VMEM((slabs, rows, LANES), F32),
            pltpu.VMEM((slabs, rows, LANES), F32),
            pltpu.VMEM((slabs, batch, LANES), F32),
        ],
        compiler_params=_seq_params(),
        name="branch_a",
    )(x, w, b, cw, cb, wg, br, bi, lam, wo)


def _branch_b_kernel(x_ref, w_ref, b_ref, cw_ref, cb_ref, lng_ref, lnb_ref, wo_ref, o_ref,
                     ubuf, vbuf, wrep, *, t, batch):
    rows = t * batch
    hist = (CONV_B - 1) * batch
    slabs = ubuf.shape[0]
    width = slabs * LANES
    reps = CONV_CHUNK // SUBLANES

    @pl.when(pl.program_id(0) == 0)
    def _():
        ubuf[:, 0:hist, :] = jnp.zeros((slabs, hist, LANES), F32)
        for j in range(CONV_B):
            wrep[j] = jnp.broadcast_to(cw_ref[j:j + 1, :], (SUBLANES, width))
        wrep[CONV_B] = jnp.broadcast_to(cb_ref[...], (SUBLANES, width))

    xb = x_ref[...].reshape(rows, x_ref.shape[2]).astype(BF16)
    p = _dot(xb, w_ref[...]) + b_ref[...]
    gb = p[:, 2 * width:]
    _store_time_major(ubuf, hist, p[:, :width] * _sigmoid(p[:, width:2 * width]), batch, t)

    def tile_rows(v):
        return jnp.concatenate([v] * reps, axis=0)

    def chunk(ci, carry):
        r0 = pl.multiple_of(ci * CONV_CHUNK, CONV_CHUNK)
        for c in range(slabs):
            lanes = _lane_block(c)
            acc = tile_rows(wrep[CONV_B, :, lanes])
            for j in range(CONV_B):
                acc = acc + tile_rows(wrep[j, :, lanes]) * ubuf[c, pl.ds(r0 + j * batch, CONV_CHUNK), :]
            vbuf[c, pl.ds(r0, CONV_CHUNK), :] = acc
        return carry

    lax.fori_loop(0, rows // CONV_CHUNK, chunk, 0)
    for c in range(slabs):
        ubuf[c, 0:hist, :] = ubuf[c, rows:rows + hist, :]

    u = _silu(_layer_norm(_load_batch_major(vbuf, batch, t), lng_ref[...], lnb_ref[...]))
    hb = u * _silu(gb)
    o_ref[...] = _dot(hb.astype(BF16), wo_ref[...]).astype(o_ref.dtype).reshape(o_ref.shape)


def _branch_b(x, w, b, cw, cb, lng, lnb, wo):
    batch, seq, d = x.shape
    width = cw.shape[1]
    slabs = width // LANES
    t = T_TILE
    rows = t * batch
    assert rows >= (CONV_B - 1) * batch and rows % CONV_CHUNK == 0
    return pl.pallas_call(
        functools.partial(_branch_b_kernel, t=t, batch=batch),
        grid=(seq // t,),
        in_specs=[pl.BlockSpec((batch, t, d), lambda i: (0, i, 0))]
        + [_full_spec(a.shape) for a in (w, b, cw, cb, lng, lnb, wo)],
        out_specs=pl.BlockSpec((batch, t, d), lambda i: (0, i, 0)),
        out_shape=jax.ShapeDtypeStruct((batch, seq, d), BF16),
        scratch_shapes=[
            pltpu.VMEM((slabs, (CONV_B - 1) * batch + rows, LANES), F32),
            pltpu.VMEM((slabs, rows, LANES), F32),
            pltpu.VMEM((CONV_B + 1, SUBLANES, width), F32),
        ],
        compiler_params=_seq_params(),
        name="branch_b",
    )(x, w, b, cw, cb, lng, lnb, wo)


def _gate_placement():
    p = np.zeros((SPLIT * LANES, LANES), np.float32)
    for hd in range(N_HEADS):
        for e in range(SPLIT):
            p[e * LANES + hd, SPLIT * hd + e] = -1.0
            p[e * LANES + hd, QUERY_LANES + SPLIT * hd + e] = 1.0
    return jnp.asarray(p, BF16)


def _proj_c_kernel(x_ref, w_ref, b_ref, wf_ref, bf_ref, place_ref, q_ref, k_ref, v_ref, cs_ref,
                   cbuf, cstate, *, t, batch, width):
    rows = t * batch

    @pl.when(pl.program_id(0) == 0)
    def _():
        cstate[...] = jnp.zeros_like(cstate)

    xb = x_ref[...].reshape(rows, x_ref.shape[2]).astype(BF16)
    p = _dot(xb, w_ref[...]) + b_ref[...]
    q_scale = LOG2E / math.sqrt(HEAD_DIM)
    q_ref[...] = (p[:, :width] * q_scale).astype(q_ref.dtype).reshape(q_ref.shape)
    k_ref[...] = p[:, width:2 * width].astype(k_ref.dtype).reshape(k_ref.shape)
    v_ref[...] = p[:, 2 * width:].astype(v_ref.dtype).reshape(v_ref.shape)

    lf = _log_sigmoid(_dot(xb, wf_ref[...]) + bf_ref[...]) * LOG2E
    _store_time_major(cbuf, 0, lf, batch, t)

    def step(tt, c):
        r0 = pl.multiple_of(tt * batch, batch)
        c = c + cbuf[0, pl.ds(r0, batch), :]
        cbuf[0, pl.ds(r0, batch), :] = c
        return c

    cstate[...] = lax.fori_loop(0, t, step, cstate[...], unroll=8)

    rem = _load_batch_major(cbuf, batch, t)
    pieces = []
    for _ in range(SPLIT):
        piece = rem.astype(BF16)
        pieces.append(piece)
        rem = rem - piece.astype(F32)
    cs = _dot(jnp.concatenate(pieces, axis=-1), place_ref[...])
    cs_ref[...] = cs.astype(cs_ref.dtype).reshape(cs_ref.shape)


def _proj_c(x, w, b, wf, bf):
    batch, seq, d = x.shape
    width = w.shape[1] // 3
    t = T_TILE
    place = _gate_placement()
    spec = lambda cols: pl.BlockSpec((batch, t, cols), lambda i: (0, i, 0))
    return pl.pallas_call(
        functools.partial(_proj_c_kernel, t=t, batch=batch, width=width),
        grid=(seq // t,),
        in_specs=[spec(d)] + [_full_spec(a.shape) for a in (w, b, wf, bf, place)],
        out_specs=[spec(width), spec(width), spec(width), spec(LANES)],
        out_shape=[
            jax.ShapeDtypeStruct((batch, seq, width), BF16),
            jax.ShapeDtypeStruct((batch, seq, width), BF16),
            jax.ShapeDtypeStruct((batch, seq, width), BF16),
            jax.ShapeDtypeStruct((batch, seq, LANES), BF16),
        ],
        scratch_shapes=[pltpu.VMEM((1, t * batch, LANES), F32), pltpu.VMEM((batch, LANES), F32)],
        compiler_params=_seq_params(),
        name="proj_c",
    )(x, w, b, wf, bf, place)


def _attn_kernel(q_ref, k_ref, v_ref, cs_ref, o_ref, qcat_t, kcat, vm_t, *bufs, tile):
    hp = pl.program_id(1)
    seq = q_ref.shape[0]
    lane = lax.broadcasted_iota(jnp.int32, (1, LANES), 1)
    sub = lax.broadcasted_iota(jnp.int32, (LANES, 1), 0)
    head_rows = (sub < HEAD_DIM, sub >= HEAD_DIM)
    sum_row = (HEAD_DIM, 0)
    zero = jnp.zeros((), BF16)
    one = jnp.ones((), BF16)
    keep = (lax.broadcasted_iota(jnp.int32, (tile, tile), 0)
            <= lax.broadcasted_iota(jnp.int32, (tile, tile), 1))

    cs = cs_ref[...]
    qt, vt, ct = q_ref[...].T, v_ref[...].T, cs.T
    kcat[:, 0:LANES] = k_ref[...]
    for h in range(2):
        lo_q = SPLIT * (2 * hp + h)
        lo_k = QUERY_LANES + lo_q
        sel_q = (sub >= lo_q) & (sub < lo_q + SPLIT)
        sel_k = (lane >= lo_k) & (lane < lo_k + SPLIT)
        qcat_t[h, 0:LANES, :] = jnp.where(head_rows[h], qt, zero)
        qcat_t[h, LANES:, :] = jnp.where(sel_q, one, jnp.where(sub >= QUERY_LANES, ct, zero))
        kcat[:, (h + 1) * LANES:(h + 2) * LANES] = jnp.where(
            sel_k, one, jnp.where(lane < QUERY_LANES, cs, zero))
        vm_t[h] = jnp.where(sub == sum_row[h], one, jnp.where(head_rows[h], vt, zero))

    def buffers(i, h):
        return bufs[2 * (i % 2) + h], bufs[4 + 2 * (i % 2) + h]

    def scores(i, h):
        q0, klen = i * tile, (i + 1) * tile
        s_buf, _ = buffers(i, h)
        keys = jnp.concatenate(
            [kcat[0:klen, 0:LANES], kcat[0:klen, (h + 1) * LANES:(h + 2) * LANES]], axis=-1)
        s_buf[0:klen, :] = _dot(keys, qcat_t[h, :, q0:klen])
        s_buf[q0:klen, :] = jnp.where(keep, s_buf[q0:klen, :], MASKED)

    def probs_and_values(i, h):
        klen = (i + 1) * tile
        s_buf, p_buf = buffers(i, h)
        m8 = None
        for c in range(klen // KEY_CHUNK):
            rows = slice(c * KEY_CHUNK, (c + 1) * KEY_CHUNK)
            part = jnp.max(s_buf[rows, :].reshape(KEY_CHUNK // SUBLANES, SUBLANES, tile), axis=0)
            m8 = part if m8 is None else jnp.maximum(m8, part)
        m = jnp.max(m8, axis=0, keepdims=True)
        for c in range(klen // KEY_CHUNK):
            rows = slice(c * KEY_CHUNK, (c + 1) * KEY_CHUNK)
            p_buf[rows, :] = jnp.exp2(s_buf[rows, :] - m).astype(BF16)
        pv = _dot(vm_t[h, :, 0:klen], p_buf[0:klen, :])
        return pv / pv[sum_row[h]:sum_row[h] + 1, :]

    units = [(i, h) for i in range(seq // tile) for h in range(2)]
    for unit in units[:AHEAD]:
        scores(*unit)
    outs = []
    for u, (i, h) in enumerate(units):
        if u + AHEAD < len(units):
            scores(*units[u + AHEAD])
        outs.append(probs_and_values(i, h))
        if h == 1:
            o_ref[i * tile:(i + 1) * tile, :] = (
                jnp.where(head_rows[0], outs[0], outs[1]).T.astype(o_ref.dtype))
            outs = []


def _attention(q, k, v, cs):
    batch, seq, width = q.shape
    tile = Q_TILE
    assert tile % KEY_CHUNK == 0
    assert QUERY_LANES + SPLIT * N_HEADS <= LANES and SPLIT * N_HEADS <= QUERY_LANES
    pair = pl.BlockSpec((None, seq, LANES), lambda b, hp: (b, 0, hp))
    return pl.pallas_call(
        functools.partial(_attn_kernel, tile=tile),
        grid=(batch, width // LANES),
        in_specs=[pair, pair, pair, pl.BlockSpec((None, seq, LANES), lambda b, hp: (b, 0, 0))],
        out_specs=pair,
        out_shape=jax.ShapeDtypeStruct((batch, seq, width), BF16),
        scratch_shapes=[
            pltpu.VMEM((2, 2 * LANES, seq), BF16),
            pltpu.VMEM((seq, 3 * LANES), BF16),
            pltpu.VMEM((2, LANES, seq), BF16),
        ] + [pltpu.VMEM((seq, tile), F32)] * 4 + [pltpu.VMEM((seq, tile), BF16)] * 4,
        compiler_params=pltpu.CompilerParams(
            dimension_semantics=("arbitrary", "arbitrary"), vmem_limit_bytes=VMEM_LIMIT),
        name="fox_attention",
    )(q, k, v, cs)


def _merge_kernel(x_ref, ya_ref, yb_ref, oc_ref, wg_ref, bg_ref, woc_ref, wo_ref, bo_ref,
                  lng_ref, lnb_ref, out_ref, *, alpha):
    d = x_ref.shape[1]
    x = x_ref[...]
    pg = _dot(x.astype(BF16), wg_ref[...]) + bg_ref[...]
    hc = oc_ref[...].astype(F32) * _silu(pg[:, :d])
    yc = _dot(hc.astype(BF16), woc_ref[...])
    mixed = (_sigmoid(pg[:, d:2 * d]) * ya_ref[...].astype(F32)
             + _sigmoid(pg[:, 2 * d:3 * d]) * yb_ref[...].astype(F32)
             + _sigmoid(pg[:, 3 * d:]) * yc)
    out = _dot(mixed.astype(BF16), wo_ref[...]) + bo_ref[...]
    out_ref[...] = _layer_norm(alpha * x + out, lng_ref[...], lnb_ref[...]).astype(out_ref.dtype)


def _merge(x, ya, yb, oc, wg, bg, woc, wo, bo, lng, lnb, *, alpha):
    batch, seq, d = x.shape
    tile = MERGE_TILE
    rows = pl.BlockSpec((None, tile, d), lambda b, i: (b, i, 0))
    return pl.pallas_call(
        functools.partial(_merge_kernel, alpha=alpha),
        grid=(batch, seq // tile),
        in_specs=[rows, rows, rows, rows]
        + [_full_spec(a.shape) for a in (wg, bg, woc, wo, bo, lng, lnb)],
        out_specs=rows,
        out_shape=jax.ShapeDtypeStruct((batch, seq, d), F32),
        compiler_params=pltpu.CompilerParams(
            dimension_semantics=("arbitrary", "arbitrary"), vmem_limit_bytes=VMEM_LIMIT),
        name="merge_out",
    )(x, ya, yb, oc, wg, bg, woc, wo, bo, lng, lnb)


def _block_diag_gates(wr, wi):
    per_tile = MXU_DIM // wr.shape[1]
    eye = jnp.eye(per_tile, dtype=wr.dtype)

    def bd(w):
        w4 = w.reshape(-1, per_tile, w.shape[1], w.shape[2])
        return jnp.einsum("gaij,ab->gaibj", w4, eye).reshape(-1, MXU_DIM, MXU_DIM)

    return jnp.concatenate([bd(wr), bd(wi)], axis=-1).astype(BF16)


def kernel(x, w_in, b_in, conv_a_w, conv_a_b, lru_wr, lru_br, lru_wi, lru_bi, lru_lambda,
           w_out_a, conv_b_w, conv_b_b, lnb_g, lnb_b, w_out_b, w_out_c, w_o, b_o, ln_g, ln_b):
    batch, seq, d = x.shape
    depth = w_in.shape[0]
    assert batch == SUBLANES and d == N_HEADS * HEAD_DIM
    alpha = float((2 * depth) ** 0.25)
    row = lambda a: a.reshape(1, -1)

    o_a, o_b, o_q, o_f, o_gc = 0, 2 * d, 5 * d, 8 * d, 8 * d + N_HEADS

    for l in range(depth):
        w, b = w_in[l], b_in[l]
        wb = lambda lo, hi: w[:, lo:hi].astype(BF16)
        bb = lambda lo, hi: row(b[lo:hi])

        ya = _branch_a(x, wb(o_a, o_b), bb(o_a, o_b), conv_a_w[l], row(conv_a_b[l]),
                       _block_diag_gates(lru_wr[l], lru_wi[l]), row(lru_br[l]), row(lru_bi[l]),
                       row(lru_lambda[l]), w_out_a[l].astype(BF16))
        yb = _branch_b(x, wb(o_b, o_q), bb(o_b, o_q), conv_b_w[l], row(conv_b_b[l]),
                       row(lnb_g[l]), row(lnb_b[l]), w_out_b[l].astype(BF16))

        wf = jnp.pad(w[:, o_f:o_gc], ((0, 0), (0, LANES - N_HEADS))).astype(BF16)
        bf = jnp.pad(b[o_f:o_gc], (0, LANES - N_HEADS)).reshape(1, LANES)
        q, k, v, cs = _proj_c(x, wb(o_q, o_f), bb(o_q, o_f), wf, bf)
        oc = _attention(q, k, v, cs)

        x = _merge(x, ya, yb, oc, wb(o_gc, w.shape[1]), bb(o_gc, w.shape[1]),
                   w_out_c[l].astype(BF16), w_o[l].astype(BF16), row(b_o[l]),
                   row(ln_g[l]), row(ln_b[l]), alpha=alpha)
    return x
```

```python
import functools
import math

import jax
import jax.numpy as jnp
import numpy as np
from jax import lax
from jax.experimental import pallas as pl
from jax.experimental.pallas import tpu as pltpu

F32 = jnp.float32
BF16 = jnp.bfloat16

LRU_C = 8.0
LN_EPS = 1e-5
CONV_A = 4
CONV_B = 31
N_HEADS = 16
HEAD_DIM = 64
LANES = 128
SUBLANES = 8
MXU_DIM = 256
VMEM_LIMIT = 56 * 1024 * 1024
LOG2E = math.log2(math.e)
MASKED = -1e30
SPLIT = 3
QUERY_LANES = 64

T_TILE = 64
CONV_CHUNK = 64
Q_TILE = 256
KEY_CHUNK = 128
AHEAD = 3
MERGE_TILE = 512


def _sigmoid(x):
    return 0.5 * jnp.tanh(0.5 * x) + 0.5


def _silu(x):
    return x * _sigmoid(x)


def _log_sigmoid(x):
    return jnp.minimum(x, 0.0) - jnp.log1p(jnp.exp(-jnp.abs(x)))


def _layer_norm(x, g, b):
    mu = jnp.mean(x, axis=-1, keepdims=True)
    xc = x - mu
    var = jnp.mean(xc * xc, axis=-1, keepdims=True)
    return xc * lax.rsqrt(var + LN_EPS) * g + b


def _dot(a, b):
    return jnp.dot(a, b, preferred_element_type=F32)


def _lane_block(c):
    return slice(c * LANES, (c + 1) * LANES)


def _store_time_major(dst, row0, val, batch, t):
    for c in range(dst.shape[0]):
        for b in range(batch):
            dst[c, pl.ds(row0 + b, t, stride=batch), :] = val[b * t:(b + 1) * t, _lane_block(c)]


def _load_batch_major(src, batch, t):
    return jnp.concatenate(
        [jnp.concatenate([src[c, pl.ds(b, t, stride=batch), :] for c in range(src.shape[0])], axis=-1)
         for b in range(batch)], axis=0)


def _full_spec(shape):
    return pl.BlockSpec(shape, lambda *_: (0,) * len(shape))


def _seq_params():
    return pltpu.CompilerParams(dimension_semantics=("arbitrary",), vmem_limit_bytes=VMEM_LIMIT)


def _branch_a_kernel(x_ref, w_ref, b_ref, cw_ref, cb_ref, wg_ref, br_ref, bi_ref, lam_ref,
                     wo_ref, o_ref, xbuf, a_buf, h_buf, hstate, *, t, batch):
    rows = t * batch
    hist = (CONV_A - 1) * batch
    slabs = xbuf.shape[0]
    width = slabs * LANES

    @pl.when(pl.program_id(0) == 0)
    def _():
        xbuf[:, 0:hist, :] = jnp.zeros((slabs, hist, LANES), F32)
        hstate[...] = jnp.zeros_like(hstate)

    xb = x_ref[...].reshape(rows, x_ref.shape[2]).astype(BF16)
    p = _dot(xb, w_ref[...]) + b_ref[...]
    ga = p[:, width:]
    _store_time_major(xbuf, hist, p[:, :width], batch, t)

    ys = []
    for c in range(slabs):
        lanes = _lane_block(c)
        y = cb_ref[:, lanes] + cw_ref[0:1, lanes] * xbuf[c, 0:rows, :]
        for j in range(1, CONV_A):
            y = y + cw_ref[j:j + 1, lanes] * xbuf[c, j * batch:j * batch + rows, :]
        xbuf[c, 0:hist, :] = xbuf[c, rows:rows + hist, :]
        ys.append(y)

    log_sig = _log_sigmoid(lam_ref[...])
    per_tile = MXU_DIM // LANES
    for g in range(width // MXU_DIM):
        yb = jnp.concatenate(ys[g * per_tile:(g + 1) * per_tile], axis=-1).astype(BF16)
        gates = _dot(yb, wg_ref[g])
        for s in range(per_tile):
            c = g * per_tile + s
            lanes = _lane_block(c)
            r = _sigmoid(gates[:, _lane_block(s)] + br_ref[:, lanes])
            ig = _sigmoid(gates[:, _lane_block(per_tile + s)] + bi_ref[:, lanes])
            log_a = (LRU_C * r) * log_sig[:, lanes]
            a = jnp.exp(log_a)
            gain = jnp.sqrt(-jnp.tanh(log_a) * (1.0 + a * a))
            a_buf[c] = a
            h_buf[c] = gain * (ig * ys[c])

    def step(tt, hs):
        r0 = pl.multiple_of(tt * batch, batch)
        out = []
        for c in range(slabs):
            h = a_buf[c, pl.ds(r0, batch), :] * hs[c] + h_buf[c, pl.ds(r0, batch), :]
            h_buf[c, pl.ds(r0, batch), :] = h
            out.append(h)
        return tuple(out)

    hs = lax.fori_loop(0, t, step, tuple(hstate[c] for c in range(slabs)), unroll=8)
    for c in range(slabs):
        hstate[c] = hs[c]

    ha = _load_batch_major(h_buf, batch, t) * _silu(ga)
    o_ref[...] = _dot(ha.astype(BF16), wo_ref[...]).astype(o_ref.dtype).reshape(o_ref.shape)


def _branch_a(x, w, b, cw, cb, wg, br, bi, lam, wo):
    batch, seq, d = x.shape
    width = cw.shape[1]
    slabs = width // LANES
    t = T_TILE
    rows = t * batch
    return pl.pallas_call(
        functools.partial(_branch_a_kernel, t=t, batch=batch),
        grid=(seq // t,),
        in_specs=[pl.BlockSpec((batch, t, d), lambda i: (0, i, 0))]
        + [_full_spec(a.shape) for a in (w, b, cw, cb, wg, br, bi, lam, wo)],
        out_specs=pl.BlockSpec((batch, t, d), lambda i: (0, i, 0)),
        out_shape=jax.ShapeDtypeStruct((batch, seq, d), BF16),
        scratch_shapes=[
            pltpu.VMEM((slabs, (CONV_A - 1) * batch + rows, LANES), F32),
            pltpu.VMEM((slabs, rows, LANES), F32),
            pltpu.VMEM((slabs, rows, LANES), F32),
            pltpu.VMEM((slabs, batch, LANES), F32),
        ],
        compiler_params=_seq_params(),
        name="branch_a",
    )(x, w, b, cw, cb, wg, br, bi, lam, wo)


def _gate_placement():
    p = np.zeros((SPLIT * LANES, LANES), np.float32)
    for hd in range(N_HEADS):
        for e in range(SPLIT):
            p[e * LANES + hd, SPLIT * hd + e] = -1.0
            p[e * LANES + hd, QUERY_LANES + SPLIT * hd + e] = 1.0
    return jnp.asarray(p, BF16)


def _branch_bc_kernel(x_ref, w_ref, b_ref, cw_ref, cb_ref, lng_ref, lnb_ref, wo_ref,
                      wc_ref, bc_ref, wf_ref, bf_ref, place_ref,
                      o_ref, q_ref, k_ref, v_ref, cs_ref,
                      ubuf, vbuf, gbuf, hbuf, wrep, cbuf, cstate, cbefore, *, t, batch, n_tiles):
    i = pl.program_id(0)
    rows = t * batch
    hist = (CONV_B - 1) * batch
    slot_rows = hist + rows
    slabs = ubuf.shape[0]
    width = slabs * LANES
    reps = CONV_CHUNK // SUBLANES
    per_group = MXU_DIM // LANES
    cur = lax.rem(i, 2)
    prev = 1 - cur
    cur0 = pl.multiple_of(cur * slot_rows, SUBLANES)
    prev0 = pl.multiple_of(prev * slot_rows, SUBLANES)

    @pl.when(i == 0)
    def _():
        ubuf[...] = jnp.zeros_like(ubuf)
        gbuf[...] = jnp.zeros_like(gbuf)
        hbuf[...] = jnp.zeros_like(hbuf)
        cstate[...] = jnp.zeros_like(cstate)
        cbefore[...] = jnp.zeros_like(cbefore)
        for j in range(CONV_B):
            wrep[j] = jnp.broadcast_to(cw_ref[j:j + 1, :], (SUBLANES, width))
        wrep[CONV_B] = jnp.broadcast_to(cb_ref[...], (SUBLANES, width))

    o_ref[...] = _dot(hbuf[...], wo_ref[...]).astype(o_ref.dtype).reshape(o_ref.shape)

    xb = x_ref[...].reshape(rows, x_ref.shape[2]).astype(BF16)

    def proj(w, b, off, g):
        sl = slice(off + g * MXU_DIM, off + (g + 1) * MXU_DIM)
        return _dot(xb, w[:, sl]) + b[:, sl]

    def conv_slab(c):
        lanes = _lane_block(c)
        for r0 in range(0, rows, CONV_CHUNK):
            acc = jnp.concatenate([wrep[CONV_B, :, lanes]] * reps, axis=0)
            for j in range(CONV_B):
                acc = acc + jnp.concatenate([wrep[j, :, lanes]] * reps, axis=0) * ubuf[
                    c, pl.ds(prev0 + r0 + j * batch, CONV_CHUNK), :]
            vbuf[c, r0:r0 + CONV_CHUNK, :] = acc

    def fill_slab(c, u):
        for bb in range(batch):
            ubuf[c, pl.ds(cur0 + hist + bb, t, stride=batch), :] = u[bb * t:(bb + 1) * t, :]
        ubuf[c, pl.ds(cur0, hist), :] = ubuf[c, pl.ds(prev0 + rows, hist), :]

    def qkv_group(ref, which, g, scale):
        val = proj(wc_ref, bc_ref, which * width, g)
        if scale != 1.0:
            val = val * scale
        ref[:, :, g * MXU_DIM:(g + 1) * MXU_DIM] = val.astype(ref.dtype).reshape(batch, t, MXU_DIM)

    q_scale = LOG2E / math.sqrt(HEAD_DIM)
    for g in range(width // MXU_DIM):
        conv_slab(g * per_group)
        u = proj(w_ref, b_ref, 0, g) * _sigmoid(proj(w_ref, b_ref, width, g))
        gbuf[cur, :, g * MXU_DIM:(g + 1) * MXU_DIM] = proj(w_ref, b_ref, 2 * width, g)
        for s in range(per_group):
            fill_slab(g * per_group + s, u[:, _lane_block(s)])
        qkv_group(q_ref, 0, g, q_scale)
        conv_slab(g * per_group + 1)
        qkv_group(k_ref, 1, g, 1.0)
        qkv_group(v_ref, 2, g, 1.0)

    lf = _log_sigmoid(_dot(xb, wf_ref[...]) + bf_ref[...]) * LOG2E
    _store_time_major(cbuf, 0, lf, batch, t)
    start = jnp.where(i < n_tiles, cstate[...], cbefore[...])
    cbefore[...] = start
    c = start
    for tt in range(t):
        c = c + cbuf[0, tt * batch:(tt + 1) * batch, :]
        cbuf[0, tt * batch:(tt + 1) * batch, :] = c
    cstate[...] = c
    rem = _load_batch_major(cbuf, batch, t)
    pieces = []
    for _ in range(SPLIT):
        piece = rem.astype(BF16)
        pieces.append(piece)
        rem = rem - piece.astype(F32)
    cs = _dot(jnp.concatenate(pieces, axis=-1), place_ref[...])
    cs_ref[...] = cs.astype(cs_ref.dtype).reshape(cs_ref.shape)

    y = _silu(_layer_norm(_load_batch_major(vbuf, batch, t), lng_ref[...], lnb_ref[...]))
    hbuf[...] = (y * _silu(gbuf[prev])).astype(hbuf.dtype)


def _branch_bc(x, w, b, cw, cb, lng, lnb, wo, wc, bc, wf, bf):
    batch, seq, d = x.shape
    width = cw.shape[1]
    slabs = width // LANES
    t = T_TILE
    rows = t * batch
    n = seq // t
    assert rows >= (CONV_B - 1) * batch and rows % CONV_CHUNK == 0
    place = _gate_placement()
    now = lambda cols: pl.BlockSpec((batch, t, cols), lambda i: (0, jnp.minimum(i, n - 1), 0))
    lag = lambda cols: pl.BlockSpec((batch, t, cols), lambda i: (0, jnp.maximum(i - 2, 0), 0))
    act = lambda cols: jax.ShapeDtypeStruct((batch, seq, cols), BF16)
    return pl.pallas_call(
        functools.partial(_branch_bc_kernel, t=t, batch=batch, n_tiles=n),
        grid=(n + 2,),
        in_specs=[now(d)]
        + [_full_spec(a.shape) for a in (w, b, cw, cb, lng, lnb, wo, wc, bc, wf, bf, place)],
        out_specs=[lag(d), now(width), now(width), now(width), now(LANES)],
        out_shape=[act(d), act(width), act(width), act(width), act(LANES)],
        scratch_shapes=[
            pltpu.VMEM((slabs, 2 * ((CONV_B - 1) * batch + rows), LANES), F32),
            pltpu.VMEM((slabs, rows, LANES), F32),
            pltpu.VMEM((2, rows, width), F32),
            pltpu.VMEM((rows, width), BF16),
            pltpu.VMEM((CONV_B + 1, SUBLANES, width), F32),
            pltpu.VMEM((1, rows, LANES), F32),
            pltpu.VMEM((batch, LANES), F32),
            pltpu.VMEM((batch, LANES), F32),
        ],
        compiler_params=_seq_params(),
        name="branch_bc",
    )(x, w, b, cw, cb, lng, lnb, wo, wc, bc, wf, bf, place)


def _attn_kernel(q_ref, k_ref, v_ref, cs_ref, o_ref, qcat_t, kcat, vm_t, *bufs, tile):
    hp = pl.program_id(1)
    seq = q_ref.shape[0]
    lane = lax.broadcasted_iota(jnp.int32, (1, LANES), 1)
    sub = lax.broadcasted_iota(jnp.int32, (LANES, 1), 0)
    head_rows = (sub < HEAD_DIM, sub >= HEAD_DIM)
    sum_row = (HEAD_DIM, 0)
    zero = jnp.zeros((), BF16)
    one = jnp.ones((), BF16)
    keep = (lax.broadcasted_iota(jnp.int32, (tile, tile), 0)
            <= lax.broadcasted_iota(jnp.int32, (tile, tile), 1))

    cs = cs_ref[...]
    qt, vt, ct = q_ref[...].T, v_ref[...].T, cs.T
    kcat[:, 0:LANES] = k_ref[...]
    for h in range(2):
        lo_q = SPLIT * (2 * hp + h)
        lo_k = QUERY_LANES + lo_q
        sel_q = (sub >= lo_q) & (sub < lo_q + SPLIT)
        sel_k = (lane >= lo_k) & (lane < lo_k + SPLIT)
        qcat_t[h, 0:LANES, :] = jnp.where(head_rows[h], qt, zero)
        qcat_t[h, LANES:, :] = jnp.where(sel_q, one, jnp.where(sub >= QUERY_LANES, ct, zero))
        kcat[:, (h + 1) * LANES:(h + 2) * LANES] = jnp.where(
            sel_k, one, jnp.where(lane < QUERY_LANES, cs, zero))
        vm_t[h] = jnp.where(sub == sum_row[h], one, jnp.where(head_rows[h], vt, zero))

    def buffers(i, h):
        return bufs[2 * (i % 2) + h], bufs[4 + 2 * (i % 2) + h]

    def scores(i, h):
        q0, klen = i * tile, (i + 1) * tile
        s_buf, _ = buffers(i, h)
        keys = jnp.concatenate(
            [kcat[0:klen, 0:LANES], kcat[0:klen, (h + 1) * LANES:(h + 2) * LANES]], axis=-1)
        s_buf[0:klen, :] = _dot(keys, qcat_t[h, :, q0:klen])
        s_buf[q0:klen, :] = jnp.where(keep, s_buf[q0:klen, :], MASKED)

    def probs_and_values(i, h):
        klen = (i + 1) * tile
        s_buf, p_buf = buffers(i, h)
        m8 = None
        for c in range(klen // KEY_CHUNK):
            rows = slice(c * KEY_CHUNK, (c + 1) * KEY_CHUNK)
            part = jnp.max(s_buf[rows, :].reshape(KEY_CHUNK // SUBLANES, SUBLANES, tile), axis=0)
            m8 = part if m8 is None else jnp.maximum(m8, part)
        m = jnp.max(m8, axis=0, keepdims=True)
        for c in range(klen // KEY_CHUNK):
            rows = slice(c * KEY_CHUNK, (c + 1) * KEY_CHUNK)
            p_buf[rows, :] = jnp.exp2(s_buf[rows, :] - m).astype(BF16)
        pv = _dot(vm_t[h, :, 0:klen], p_buf[0:klen, :])
        return pv / pv[sum_row[h]:sum_row[h] + 1, :]

    units = [(i, h) for i in range(seq // tile) for h in range(2)]
    for unit in units[:AHEAD]:
        scores(*unit)
    outs = []
    for u, (i, h) in enumerate(units):
        if u + AHEAD < len(units):
            scores(*units[u + AHEAD])
        outs.append(probs_and_values(i, h))
        if h == 1:
            o_ref[i * tile:(i + 1) * tile, :] = (
                jnp.where(head_rows[0], outs[0], outs[1]).T.astype(o_ref.dtype))
            outs = []


def _attention(q, k, v, cs):
    batch, seq, width = q.shape
    tile = Q_TILE
    assert tile % KEY_CHUNK == 0
    assert QUERY_LANES + SPLIT * N_HEADS <= LANES and SPLIT * N_HEADS <= QUERY_LANES
    pair = pl.BlockSpec((None, seq, LANES), lambda b, hp: (b, 0, hp))
    return pl.pallas_call(
        functools.partial(_attn_kernel, tile=tile),
        grid=(batch, width // LANES),
        in_specs=[pair, pair, pair, pl.BlockSpec((None, seq, LANES), lambda b, hp: (b, 0, 0))],
        out_specs=pair,
        out_shape=jax.ShapeDtypeStruct((batch, seq, width), BF16),
        scratch_shapes=[
            pltpu.VMEM((2, 2 * LANES, seq), BF16),
            pltpu.VMEM((seq, 3 * LANES), BF16),
            pltpu.VMEM((2, LANES, seq), BF16),
        ] + [pltpu.VMEM((seq, tile), F32)] * 4 + [pltpu.VMEM((seq, tile), BF16)] * 4,
        compiler_params=pltpu.CompilerParams(
            dimension_semantics=("arbitrary", "arbitrary"), vmem_limit_bytes=VMEM_LIMIT),
        name="fox_attention",
    )(q, k, v, cs)


def _merge_kernel(x_ref, ya_ref, yb_ref, oc_ref, wg_ref, bg_ref, woc_ref, wo_ref, bo_ref,
                  lng_ref, lnb_ref, out_ref, *, alpha):
    d = x_ref.shape[1]
    x = x_ref[...]
    pg = _dot(x.astype(BF16), wg_ref[...]) + bg_ref[...]
    hc = oc_ref[...].astype(F32) * _silu(pg[:, :d])
    yc = _dot(hc.astype(BF16), woc_ref[...])
    mixed = (_sigmoid(pg[:, d:2 * d]) * ya_ref[...].astype(F32)
             + _sigmoid(pg[:, 2 * d:3 * d]) * yb_ref[...].astype(F32)
             + _sigmoid(pg[:, 3 * d:]) * yc)
    out = _dot(mixed.astype(BF16), wo_ref[...]) + bo_ref[...]
    out_ref[...] = _layer_norm(alpha * x + out, lng_ref[...], lnb_ref[...]).astype(out_ref.dtype)


def _merge(x, ya, yb, oc, wg, bg, woc, wo, bo, lng, lnb, *, alpha):
    batch, seq, d = x.shape
    tile = MERGE_TILE
    rows = pl.BlockSpec((None, tile, d), lambda b, i: (b, i, 0))
    return pl.pallas_call(
        functools.partial(_merge_kernel, alpha=alpha),
        grid=(batch, seq // tile),
        in_specs=[rows, rows, rows, rows]
        + [_full_spec(a.shape) for a in (wg, bg, woc, wo, bo, lng, lnb)],
        out_specs=rows,
        out_shape=jax.ShapeDtypeStruct((batch, seq, d), F32),
        compiler_params=pltpu.CompilerParams(
            dimension_semantics=("arbitrary", "arbitrary"), vmem_limit_bytes=VMEM_LIMIT),
        name="merge_out",
    )(x, ya, yb, oc, wg, bg, woc, wo, bo, lng, lnb)


def _block_diag_gates(wr, wi):
    per_tile = MXU_DIM // wr.shape[1]
    eye = jnp.eye(per_tile, dtype=wr.dtype)

    def bd(w):
        w4 = w.reshape(-1, per_tile, w.shape[1], w.shape[2])
        return jnp.einsum("gaij,ab->gaibj", w4, eye).reshape(-1, MXU_DIM, MXU_DIM)

    return jnp.concatenate([bd(wr), bd(wi)], axis=-1).astype(BF16)


def kernel(x, w_in, b_in, conv_a_w, conv_a_b, lru_wr, lru_br, lru_wi, lru_bi, lru_lambda,
           w_out_a, conv_b_w, conv_b_b, lnb_g, lnb_b, w_out_b, w_out_c, w_o, b_o, ln_g, ln_b):
    batch, seq, d = x.shape
    depth = w_in.shape[0]
    assert batch == SUBLANES and d == N_HEADS * HEAD_DIM
    alpha = float((2 * depth) ** 0.25)
    row = lambda a: a.reshape(1, -1)

    o_a, o_b, o_q, o_f, o_gc = 0, 2 * d, 5 * d, 8 * d, 8 * d + N_HEADS

    for l in range(depth):
        w, b = w_in[l], b_in[l]
        wb = lambda lo, hi: w[:, lo:hi].astype(BF16)
        bb = lambda lo, hi: row(b[lo:hi])

        ya = _branch_a(x, wb(o_a, o_b), bb(o_a, o_b), conv_a_w[l], row(conv_a_b[l]),
                       _block_diag_gates(lru_wr[l], lru_wi[l]), row(lru_br[l]), row(lru_bi[l]),
                       row(lru_lambda[l]), w_out_a[l].astype(BF16))
        wf = jnp.pad(w[:, o_f:o_gc], ((0, 0), (0, LANES - N_HEADS))).astype(BF16)
        bf = jnp.pad(b[o_f:o_gc], (0, LANES - N_HEADS)).reshape(1, LANES)
        yb, q, k, v, cs = _branch_bc(x, wb(o_b, o_q), bb(o_b, o_q), conv_b_w[l], row(conv_b_b[l]),
                                     row(lnb_g[l]), row(lnb_b[l]), w_out_b[l].astype(BF16),
                                     wb(o_q, o_f), bb(o_q, o_f), wf, bf)
        oc = _attention(q, k, v, cs)

        x = _merge(x, ya, yb, oc, wb(o_gc, w.shape[1]), bb(o_gc, w.shape[1]),
                   w_out_c[l].astype(BF16), w_o[l].astype(BF16), row(b_o[l]),
                   row(ln_g[l]), row(ln_b[l]), alpha=alpha)
    return x
```

```python
import functools
import math

import jax
import jax.numpy as jnp
import numpy as np
from jax import lax
from jax.experimental import pallas as pl
from jax.experimental.pallas import tpu as pltpu

F32 = jnp.float32
BF16 = jnp.bfloat16

LRU_C = 8.0
LN_EPS = 1e-5
CONV_A = 4
CONV_B = 31
N_HEADS = 16
HEAD_DIM = 64
LANES = 128
SUBLANES = 8
MXU_DIM = 256
VMEM_LIMIT = 56 * 1024 * 1024
LOG2E = math.log2(math.e)
MASKED = -1e30
SPLIT = 3
QUERY_LANES = 64

T_TILE = 64
CONV_CHUNK = 64
Q_TILE = 256
KEY_CHUNK = 128
AHEAD = 3
MERGE_TILE = 512


def _sigmoid(x):
    return 0.5 * jnp.tanh(0.5 * x) + 0.5


def _silu(x):
    return x * _sigmoid(x)


def _log_sigmoid(x):
    return jnp.minimum(x, 0.0) - jnp.log1p(jnp.exp(-jnp.abs(x)))


def _layer_norm(x, g, b):
    mu = jnp.mean(x, axis=-1, keepdims=True)
    xc = x - mu
    var = jnp.mean(xc * xc, axis=-1, keepdims=True)
    return xc * lax.rsqrt(var + LN_EPS) * g + b


def _dot(a, b):
    return jnp.dot(a, b, preferred_element_type=F32)


def _lane_block(c):
    return slice(c * LANES, (c + 1) * LANES)


def _store_time_major(dst, row0, val, batch, t):
    for c in range(dst.shape[0]):
        for b in range(batch):
            dst[c, pl.ds(row0 + b, t, stride=batch), :] = val[b * t:(b + 1) * t, _lane_block(c)]


def _load_batch_major(src, batch, t):
    return jnp.concatenate(
        [jnp.concatenate([src[c, pl.ds(b, t, stride=batch), :] for c in range(src.shape[0])], axis=-1)
         for b in range(batch)], axis=0)


def _full_spec(shape):
    return pl.BlockSpec(shape, lambda *_: (0,) * len(shape))


def _col_spec(arr, layer, cols, block):
    return pl.BlockSpec((None, arr.shape[1], cols), lambda *_: (layer, 0, block))


def _seq_params():
    return pltpu.CompilerParams(dimension_semantics=("arbitrary",), vmem_limit_bytes=VMEM_LIMIT)


def _branch_a_kernel(x_ref, w_ref, b_ref, cw_ref, cb_ref, wg_ref, br_ref, bi_ref, lam_ref,
                     wo_ref, o_ref, xbuf, a_buf, h_buf, hstate, *, t, batch):
    rows = t * batch
    hist = (CONV_A - 1) * batch
    slabs = xbuf.shape[0]
    width = slabs * LANES

    @pl.when(pl.program_id(0) == 0)
    def _():
        xbuf[:, 0:hist, :] = jnp.zeros((slabs, hist, LANES), F32)
        hstate[...] = jnp.zeros_like(hstate)

    xb = x_ref[...].reshape(rows, x_ref.shape[2]).astype(BF16)
    p = _dot(xb, w_ref[...]) + b_ref[...]
    ga = p[:, width:]
    _store_time_major(xbuf, hist, p[:, :width], batch, t)

    ys = []
    for c in range(slabs):
        lanes = _lane_block(c)
        y = cb_ref[:, lanes] + cw_ref[0:1, lanes] * xbuf[c, 0:rows, :]
        for j in range(1, CONV_A):
            y = y + cw_ref[j:j + 1, lanes] * xbuf[c, j * batch:j * batch + rows, :]
        xbuf[c, 0:hist, :] = xbuf[c, rows:rows + hist, :]
        ys.append(y)

    log_sig = _log_sigmoid(lam_ref[...])
    per_tile = MXU_DIM // LANES
    for g in range(width // MXU_DIM):
        yb = jnp.concatenate(ys[g * per_tile:(g + 1) * per_tile], axis=-1).astype(BF16)
        gates = _dot(yb, wg_ref[g])
        for s in range(per_tile):
            c = g * per_tile + s
            lanes = _lane_block(c)
            r = _sigmoid(gates[:, _lane_block(s)] + br_ref[:, lanes])
            ig = _sigmoid(gates[:, _lane_block(per_tile + s)] + bi_ref[:, lanes])
            log_a = (LRU_C * r) * log_sig[:, lanes]
            a = jnp.exp(log_a)
            gain = jnp.sqrt(-jnp.tanh(log_a) * (1.0 + a * a))
            a_buf[c] = a
            h_buf[c] = gain * (ig * ys[c])

    def step(tt, hs):
        r0 = pl.multiple_of(tt * batch, batch)
        out = []
        for c in range(slabs):
            h = a_buf[c, pl.ds(r0, batch), :] * hs[c] + h_buf[c, pl.ds(r0, batch), :]
            h_buf[c, pl.ds(r0, batch), :] = h
            out.append(h)
        return tuple(out)

    hs = lax.fori_loop(0, t, step, tuple(hstate[c] for c in range(slabs)), unroll=8)
    for c in range(slabs):
        hstate[c] = hs[c]

    ha = _load_batch_major(h_buf, batch, t) * _silu(ga)
    o_ref[...] = _dot(ha.astype(BF16), wo_ref[...]).astype(o_ref.dtype).reshape(o_ref.shape)


def _branch_a(x, w_in, b_in, layer, cw, cb, wg, br, bi, lam, wo):
    batch, seq, d = x.shape
    width = cw.shape[1]
    slabs = width // LANES
    t = T_TILE
    rows = t * batch
    return pl.pallas_call(
        functools.partial(_branch_a_kernel, t=t, batch=batch),
        grid=(seq // t,),
        in_specs=[pl.BlockSpec((batch, t, d), lambda i: (0, i, 0)),
                  _col_spec(w_in, layer, 2 * width, 0), _col_spec(b_in, layer, 2 * width, 0)]
        + [_full_spec(a.shape) for a in (cw, cb, wg, br, bi, lam, wo)],
        out_specs=pl.BlockSpec((batch, t, d), lambda i: (0, i, 0)),
        out_shape=jax.ShapeDtypeStruct((batch, seq, d), BF16),
        scratch_shapes=[
            pltpu.VMEM((slabs, (CONV_A - 1) * batch + rows, LANES), F32),
            pltpu.VMEM((slabs, rows, LANES), F32),
            pltpu.VMEM((slabs, rows, LANES), F32),
            pltpu.VMEM((slabs, batch, LANES), F32),
        ],
        compiler_params=_seq_params(),
        name="branch_a",
    )(x, w_in, b_in, cw, cb, wg, br, bi, lam, wo)


def _gate_placement():
    p = np.zeros((SPLIT * LANES, LANES), np.float32)
    for hd in range(N_HEADS):
        for e in range(SPLIT):
            p[e * LANES + hd, SPLIT * hd + e] = -1.0
            p[e * LANES + hd, QUERY_LANES + SPLIT * hd + e] = 1.0
    return jnp.asarray(p, BF16)


def _branch_bc_kernel(x_ref, wu_ref, wgate_ref, wgb_ref, wq_ref, wk_ref, wv_ref, wf_ref,
                      bu_ref, bgate_ref, bgb_ref, bq_ref, bk_ref, bv_ref, bf_ref,
                      cw_ref, cb_ref, lng_ref, lnb_ref, wo_ref, place_ref,
                      o_ref, q_ref, k_ref, v_ref, cs_ref,
                      ubuf, vbuf, gbuf, hbuf, wrep, cbuf, cstate, cbefore, *, t, batch, n_tiles):
    i = pl.program_id(0)
    rows = t * batch
    hist = (CONV_B - 1) * batch
    slot_rows = hist + rows
    slabs = ubuf.shape[0]
    width = slabs * LANES
    reps = CONV_CHUNK // SUBLANES
    per_group = MXU_DIM // LANES
    cur = lax.rem(i, 2)
    prev = 1 - cur
    cur0 = pl.multiple_of(cur * slot_rows, SUBLANES)
    prev0 = pl.multiple_of(prev * slot_rows, SUBLANES)

    @pl.when(i == 0)
    def _():
        ubuf[...] = jnp.zeros_like(ubuf)
        gbuf[...] = jnp.zeros_like(gbuf)
        hbuf[...] = jnp.zeros_like(hbuf)
        cstate[...] = jnp.zeros_like(cstate)
        cbefore[...] = jnp.zeros_like(cbefore)
        for j in range(CONV_B):
            wrep[j] = jnp.broadcast_to(cw_ref[j:j + 1, :], (SUBLANES, width))
        wrep[CONV_B] = jnp.broadcast_to(cb_ref[...], (SUBLANES, width))

    o_ref[...] = _dot(hbuf[...], wo_ref[...]).astype(o_ref.dtype).reshape(o_ref.shape)

    xb = x_ref[...].reshape(rows, x_ref.shape[2]).astype(BF16)

    def proj(w, b, g):
        sl = slice(g * MXU_DIM, (g + 1) * MXU_DIM)
        return _dot(xb, w[:, sl]) + b[:, sl]

    def conv_slab(c):
        lanes = _lane_block(c)
        for r0 in range(0, rows, CONV_CHUNK):
            acc = jnp.concatenate([wrep[CONV_B, :, lanes]] * reps, axis=0)
            for j in range(CONV_B):
                acc = acc + jnp.concatenate([wrep[j, :, lanes]] * reps, axis=0) * ubuf[
                    c, pl.ds(prev0 + r0 + j * batch, CONV_CHUNK), :]
            vbuf[c, r0:r0 + CONV_CHUNK, :] = acc

    def fill_slab(c, u):
        for bb in range(batch):
            ubuf[c, pl.ds(cur0 + hist + bb, t, stride=batch), :] = u[bb * t:(bb + 1) * t, :]
        ubuf[c, pl.ds(cur0, hist), :] = ubuf[c, pl.ds(prev0 + rows, hist), :]

    def qkv_group(ref, w, b, g, scale):
        val = proj(w, b, g)
        if scale != 1.0:
            val = val * scale
        ref[:, :, g * MXU_DIM:(g + 1) * MXU_DIM] = val.astype(ref.dtype).reshape(batch, t, MXU_DIM)

    q_scale = LOG2E / math.sqrt(HEAD_DIM)
    for g in range(width // MXU_DIM):
        conv_slab(g * per_group)
        u = proj(wu_ref, bu_ref, g) * _sigmoid(proj(wgate_ref, bgate_ref, g))
        gbuf[cur, :, g * MXU_DIM:(g + 1) * MXU_DIM] = proj(wgb_ref, bgb_ref, g)
        for s in range(per_group):
            fill_slab(g * per_group + s, u[:, _lane_block(s)])
        qkv_group(q_ref, wq_ref, bq_ref, g, q_scale)
        conv_slab(g * per_group + 1)
        qkv_group(k_ref, wk_ref, bk_ref, g, 1.0)
        qkv_group(v_ref, wv_ref, bv_ref, g, 1.0)

    lf = _log_sigmoid(_dot(xb, wf_ref[...]) + bf_ref[...]) * LOG2E
    _store_time_major(cbuf, 0, lf, batch, t)
    start = jnp.where(i < n_tiles, cstate[...], cbefore[...])
    cbefore[...] = start
    c = start
    for tt in range(t):
        c = c + cbuf[0, tt * batch:(tt + 1) * batch, :]
        cbuf[0, tt * batch:(tt + 1) * batch, :] = c
    cstate[...] = c
    rem = _load_batch_major(cbuf, batch, t)
    pieces = []
    for _ in range(SPLIT):
        piece = rem.astype(BF16)
        pieces.append(piece)
        rem = rem - piece.astype(F32)
    cs = _dot(jnp.concatenate(pieces, axis=-1), place_ref[...])
    cs_ref[...] = cs.astype(cs_ref.dtype).reshape(cs_ref.shape)

    y = _silu(_layer_norm(_load_batch_major(vbuf, batch, t), lng_ref[...], lnb_ref[...]))
    hbuf[...] = (y * _silu(gbuf[prev])).astype(hbuf.dtype)


def _branch_bc(x, w_in, b_in, layer, cw, cb, lng, lnb, wo):
    batch, seq, d = x.shape
    width = cw.shape[1]
    assert width == d and (2 + 6) * width % LANES == 0
    f_block = (2 + 6) * width // LANES
    weights = [_col_spec(w_in, layer, width, 2 + j) for j in range(6)]
    weights.append(_col_spec(w_in, layer, LANES, f_block))
    biases = [_col_spec(b_in, layer, width, 2 + j) for j in range(6)]
    biases.append(_col_spec(b_in, layer, LANES, f_block))
    slabs = width // LANES
    t = T_TILE
    rows = t * batch
    n = seq // t
    assert rows >= (CONV_B - 1) * batch and rows % CONV_CHUNK == 0
    place = _gate_placement()
    now = lambda cols: pl.BlockSpec((batch, t, cols), lambda i: (0, jnp.minimum(i, n - 1), 0))
    lag = lambda cols: pl.BlockSpec((batch, t, cols), lambda i: (0, jnp.maximum(i - 2, 0), 0))
    act = lambda cols: jax.ShapeDtypeStruct((batch, seq, cols), BF16)
    return pl.pallas_call(
        functools.partial(_branch_bc_kernel, t=t, batch=batch, n_tiles=n),
        grid=(n + 2,),
        in_specs=[now(d)] + weights + biases
        + [_full_spec(a.shape) for a in (cw, cb, lng, lnb, wo, place)],
        out_specs=[lag(d), now(width), now(width), now(width), now(LANES)],
        out_shape=[act(d), act(width), act(width), act(width), act(LANES)],
        scratch_shapes=[
            pltpu.VMEM((slabs, 2 * ((CONV_B - 1) * batch + rows), LANES), F32),
            pltpu.VMEM((slabs, rows, LANES), F32),
            pltpu.VMEM((2, rows, width), F32),
            pltpu.VMEM((rows, width), BF16),
            pltpu.VMEM((CONV_B + 1, SUBLANES, width), F32),
            pltpu.VMEM((1, rows, LANES), F32),
            pltpu.VMEM((batch, LANES), F32),
            pltpu.VMEM((batch, LANES), F32),
        ],
        compiler_params=_seq_params(),
        name="branch_bc",
    )(x, *([w_in] * 7), *([b_in] * 7), cw, cb, lng, lnb, wo, place)


def _attn_kernel(q_ref, k_ref, v_ref, cs_ref, o_ref, qcat_t, kcat, vm_t, *bufs, tile):
    hp = pl.program_id(1)
    seq = q_ref.shape[0]
    lane = lax.broadcasted_iota(jnp.int32, (1, LANES), 1)
    sub = lax.broadcasted_iota(jnp.int32, (LANES, 1), 0)
    head_rows = (sub < HEAD_DIM, sub >= HEAD_DIM)
    sum_row = (HEAD_DIM, 0)
    zero = jnp.zeros((), BF16)
    one = jnp.ones((), BF16)
    keep = (lax.broadcasted_iota(jnp.int32, (tile, tile), 0)
            <= lax.broadcasted_iota(jnp.int32, (tile, tile), 1))

    cs = cs_ref[...]
    qt, vt, ct = q_ref[...].T, v_ref[...].T, cs.T
    kcat[:, 0:LANES] = k_ref[...]
    for h in range(2):
        lo_q = SPLIT * (2 * hp + h)
        lo_k = QUERY_LANES + lo_q
        sel_q = (sub >= lo_q) & (sub < lo_q + SPLIT)
        sel_k = (lane >= lo_k) & (lane < lo_k + SPLIT)
        qcat_t[h, 0:LANES, :] = jnp.where(head_rows[h], qt, zero)
        qcat_t[h, LANES:, :] = jnp.where(sel_q, one, jnp.where(sub >= QUERY_LANES, ct, zero))
        kcat[:, (h + 1) * LANES:(h + 2) * LANES] = jnp.where(
            sel_k, one, jnp.where(lane < QUERY_LANES, cs, zero))
        vm_t[h] = jnp.where(sub == sum_row[h], one, jnp.where(head_rows[h], vt, zero))

    def buffers(i, h):
        return bufs[2 * (i % 2) + h], bufs[4 + 2 * (i % 2) + h]

    def scores(i, h):
        q0, klen = i * tile, (i + 1) * tile
        s_buf, _ = buffers(i, h)
        keys = jnp.concatenate(
            [kcat[0:klen, 0:LANES], kcat[0:klen, (h + 1) * LANES:(h + 2) * LANES]], axis=-1)
        s_buf[0:klen, :] = _dot(keys, qcat_t[h, :, q0:klen])
        s_buf[q0:klen, :] = jnp.where(keep, s_buf[q0:klen, :], MASKED)

    def probs_and_values(i, h):
        klen = (i + 1) * tile
        s_buf, p_buf = buffers(i, h)
        m8 = None
        for c in range(klen // KEY_CHUNK):
            rows = slice(c * KEY_CHUNK, (c + 1) * KEY_CHUNK)
            part = jnp.max(s_buf[rows, :].reshape(KEY_CHUNK // SUBLANES, SUBLANES, tile), axis=0)
            m8 = part if m8 is None else jnp.maximum(m8, part)
        m = jnp.max(m8, axis=0, keepdims=True)
        for c in range(klen // KEY_CHUNK):
            rows = slice(c * KEY_CHUNK, (c + 1) * KEY_CHUNK)
            p_buf[rows, :] = jnp.exp2(s_buf[rows, :] - m).astype(BF16)
        pv = _dot(vm_t[h, :, 0:klen], p_buf[0:klen, :])
        return pv / pv[sum_row[h]:sum_row[h] + 1, :]

    units = [(i, h) for i in range(seq // tile) for h in range(2)]
    for unit in units[:AHEAD]:
        scores(*unit)
    outs = []
    for u, (i, h) in enumerate(units):
        if u + AHEAD < len(units):
            scores(*units[u + AHEAD])
        outs.append(probs_and_values(i, h))
        if h == 1:
            o_ref[i * tile:(i + 1) * tile, :] = (
                jnp.where(head_rows[0], outs[0], outs[1]).T.astype(o_ref.dtype))
            outs = []


def _attention(q, k, v, cs):
    batch, seq, width = q.shape
    tile = Q_TILE
    assert tile % KEY_CHUNK == 0
    assert QUERY_LANES + SPLIT * N_HEADS <= LANES and SPLIT * N_HEADS <= QUERY_LANES
    pair = pl.BlockSpec((None, seq, LANES), lambda b, hp: (b, 0, hp))
    return pl.pallas_call(
        functools.partial(_attn_kernel, tile=tile),
        grid=(batch, width // LANES),
        in_specs=[pair, pair, pair, pl.BlockSpec((None, seq, LANES), lambda b, hp: (b, 0, 0))],
        out_specs=pair,
        out_shape=jax.ShapeDtypeStruct((batch, seq, width), BF16),
        scratch_shapes=[
            pltpu.VMEM((2, 2 * LANES, seq), BF16),
            pltpu.VMEM((seq, 3 * LANES), BF16),
            pltpu.VMEM((2, LANES, seq), BF16),
        ] + [pltpu.VMEM((seq, tile), F32)] * 4 + [pltpu.VMEM((seq, tile), BF16)] * 4,
        compiler_params=pltpu.CompilerParams(
            dimension_semantics=("arbitrary", "arbitrary"), vmem_limit_bytes=VMEM_LIMIT),
        name="fox_attention",
    )(q, k, v, cs)


def _merge_kernel(x_ref, ya_ref, yb_ref, oc_ref, wg_ref, bg_ref, woc_ref, wo_ref, bo_ref,
                  lng_ref, lnb_ref, out_ref, *, alpha):
    d = x_ref.shape[1]
    x = x_ref[...]
    pg = _dot(x.astype(BF16), wg_ref[...]) + bg_ref[...]
    hc = oc_ref[...].astype(F32) * _silu(pg[:, :d])
    yc = _dot(hc.astype(BF16), woc_ref[...])
    mixed = (_sigmoid(pg[:, d:2 * d]) * ya_ref[...].astype(F32)
             + _sigmoid(pg[:, 2 * d:3 * d]) * yb_ref[...].astype(F32)
             + _sigmoid(pg[:, 3 * d:]) * yc)
    out = _dot(mixed.astype(BF16), wo_ref[...]) + bo_ref[...]
    out_ref[...] = _layer_norm(alpha * x + out, lng_ref[...], lnb_ref[...]).astype(out_ref.dtype)


def _merge(x, ya, yb, oc, wg, bg, woc, wo, bo, lng, lnb, *, alpha):
    batch, seq, d = x.shape
    tile = MERGE_TILE
    rows = pl.BlockSpec((None, tile, d), lambda b, i: (b, i, 0))
    return pl.pallas_call(
        functools.partial(_merge_kernel, alpha=alpha),
        grid=(batch, seq // tile),
        in_specs=[rows, rows, rows, rows]
        + [_full_spec(a.shape) for a in (wg, bg, woc, wo, bo, lng, lnb)],
        out_specs=rows,
        out_shape=jax.ShapeDtypeStruct((batch, seq, d), F32),
        compiler_params=pltpu.CompilerParams(
            dimension_semantics=("arbitrary", "arbitrary"), vmem_limit_bytes=VMEM_LIMIT),
        name="merge_out",
    )(x, ya, yb, oc, wg, bg, woc, wo, bo, lng, lnb)


def _block_diag_gates(wr, wi):
    per_tile = MXU_DIM // wr.shape[1]
    eye = jnp.eye(per_tile, dtype=wr.dtype)

    def bd(w):
        w4 = w.reshape(-1, per_tile, w.shape[1], w.shape[2])
        return jnp.einsum("gaij,ab->gaibj", w4, eye).reshape(-1, MXU_DIM, MXU_DIM)

    return jnp.concatenate([bd(wr), bd(wi)], axis=-1).astype(BF16)


def kernel(x, w_in, b_in, conv_a_w, conv_a_b, lru_wr, lru_br, lru_wi, lru_bi, lru_lambda,
           w_out_a, conv_b_w, conv_b_b, lnb_g, lnb_b, w_out_b, w_out_c, w_o, b_o, ln_g, ln_b):
    batch, seq, d = x.shape
    depth = w_in.shape[0]
    assert batch == SUBLANES and d == N_HEADS * HEAD_DIM
    alpha = float((2 * depth) ** 0.25)
    row = lambda a: a.reshape(1, -1)

    o_gc = 8 * d + N_HEADS
    w_bf = w_in.astype(BF16)
    b_3d = b_in.reshape(depth, 1, -1)

    for l in range(depth):
        ya = _branch_a(x, w_bf, b_3d, l, conv_a_w[l], row(conv_a_b[l]),
                       _block_diag_gates(lru_wr[l], lru_wi[l]), row(lru_br[l]), row(lru_bi[l]),
                       row(lru_lambda[l]), w_out_a[l].astype(BF16))
        yb, q, k, v, cs = _branch_bc(x, w_bf, b_3d, l, conv_b_w[l], row(conv_b_b[l]),
                                     row(lnb_g[l]), row(lnb_b[l]), w_out_b[l].astype(BF16))
        oc = _attention(q, k, v, cs)

        x = _merge(x, ya, yb, oc, w_bf[l, :, o_gc:], row(b_in[l, o_gc:]),
                   w_out_c[l].astype(BF16), w_o[l].astype(BF16), row(b_o[l]),
                   row(ln_g[l]), row(ln_b[l]), alpha=alpha)
    return x
```

```python
import functools
import math

import jax
import jax.numpy as jnp
import numpy as np
from jax import lax
from jax.experimental import pallas as pl
from jax.experimental.pallas import tpu as pltpu

F32 = jnp.float32
BF16 = jnp.bfloat16

LRU_C = 8.0
LN_EPS = 1e-5
CONV_A = 4
CONV_B = 31
N_HEADS = 16
HEAD_DIM = 64
LANES = 128
SUBLANES = 8
MXU_DIM = 256
VMEM_LIMIT = 56 * 1024 * 1024
LOG2E = math.log2(math.e)
MASKED = -1e30
SPLIT = 3
QUERY_LANES = 64

T_TILE = 64
CONV_CHUNK = 64
Q_TILE = 256
KEY_CHUNK = 128
AHEAD = 3
N_BUFFERS = 4
PAIRS_PER_STEP = 2
A_T_TILE = 128
MERGE_TILE = 512


def _sigmoid(x):
    return 0.5 * jnp.tanh(0.5 * x) + 0.5


def _silu(x):
    return x * _sigmoid(x)


def _log_sigmoid(x):
    return jnp.minimum(x, 0.0) - jnp.log1p(jnp.exp(-jnp.abs(x)))


def _layer_norm(x, g, b):
    mu = jnp.mean(x, axis=-1, keepdims=True)
    xc = x - mu
    var = jnp.mean(xc * xc, axis=-1, keepdims=True)
    return xc * lax.rsqrt(var + LN_EPS) * g + b


def _dot(a, b):
    return jnp.dot(a, b, preferred_element_type=F32)


def _lane_block(c):
    return slice(c * LANES, (c + 1) * LANES)


def _store_time_major(dst, row0, val, batch, t):
    for c in range(dst.shape[0]):
        for b in range(batch):
            dst[c, pl.ds(row0 + b, t, stride=batch), :] = val[b * t:(b + 1) * t, _lane_block(c)]


def _load_batch_major(src, batch, t):
    return jnp.concatenate(
        [jnp.concatenate([src[c, pl.ds(b, t, stride=batch), :] for c in range(src.shape[0])], axis=-1)
         for b in range(batch)], axis=0)


def _full_spec(shape):
    return pl.BlockSpec(shape, lambda *_: (0,) * len(shape))


def _col_spec(arr, layer, cols, block):
    return pl.BlockSpec((None, arr.shape[1], cols), lambda *_: (layer, 0, block))


def _seq_params():
    return pltpu.CompilerParams(dimension_semantics=("arbitrary",), vmem_limit_bytes=VMEM_LIMIT)


def _branch_a_kernel(x_ref, w_ref, b_ref, cw_ref, cb_ref, wg_ref, br_ref, bi_ref, lam_ref,
                     wo_ref, o_ref, xbuf, a_buf, h_buf, hstate, *, t, batch):
    rows = t * batch
    hist = (CONV_A - 1) * batch
    slabs = xbuf.shape[0]
    width = slabs * LANES

    @pl.when(pl.program_id(0) == 0)
    def _():
        xbuf[:, 0:hist, :] = jnp.zeros((slabs, hist, LANES), F32)
        hstate[...] = jnp.zeros_like(hstate)

    xb = x_ref[...].reshape(rows, x_ref.shape[2]).astype(BF16)
    p = _dot(xb, w_ref[...]) + b_ref[...]
    ga = p[:, width:]
    _store_time_major(xbuf, hist, p[:, :width], batch, t)

    ys = []
    for c in range(slabs):
        lanes = _lane_block(c)
        y = cb_ref[:, lanes] + cw_ref[0:1, lanes] * xbuf[c, 0:rows, :]
        for j in range(1, CONV_A):
            y = y + cw_ref[j:j + 1, lanes] * xbuf[c, j * batch:j * batch + rows, :]
        xbuf[c, 0:hist, :] = xbuf[c, rows:rows + hist, :]
        ys.append(y)

    log_sig = _log_sigmoid(lam_ref[...])
    per_tile = MXU_DIM // LANES
    for g in range(width // MXU_DIM):
        yb = jnp.concatenate(ys[g * per_tile:(g + 1) * per_tile], axis=-1).astype(BF16)
        gates = _dot(yb, wg_ref[g])
        for s in range(per_tile):
            c = g * per_tile + s
            lanes = _lane_block(c)
            r = _sigmoid(gates[:, _lane_block(s)] + br_ref[:, lanes])
            ig = _sigmoid(gates[:, _lane_block(per_tile + s)] + bi_ref[:, lanes])
            log_a = (LRU_C * r) * log_sig[:, lanes]
            a = jnp.exp(log_a)
            gain = jnp.sqrt(-jnp.tanh(log_a) * (1.0 + a * a))
            a_buf[c] = a
            h_buf[c] = gain * (ig * ys[c])

    def step(tt, hs):
        r0 = pl.multiple_of(tt * batch, batch)
        out = []
        for c in range(slabs):
            h = a_buf[c, pl.ds(r0, batch), :] * hs[c] + h_buf[c, pl.ds(r0, batch), :]
            h_buf[c, pl.ds(r0, batch), :] = h
            out.append(h)
        return tuple(out)

    hs = lax.fori_loop(0, t, step, tuple(hstate[c] for c in range(slabs)), unroll=8)
    for c in range(slabs):
        hstate[c] = hs[c]

    ha = _load_batch_major(h_buf, batch, t) * _silu(ga)
    o_ref[...] = _dot(ha.astype(BF16), wo_ref[...]).astype(o_ref.dtype).reshape(o_ref.shape)


def _branch_a(x, w_in, b_in, layer, cw, cb, wg, br, bi, lam, wo):
    batch, seq, d = x.shape
    width = cw.shape[1]
    slabs = width // LANES
    t = A_T_TILE
    rows = t * batch
    return pl.pallas_call(
        functools.partial(_branch_a_kernel, t=t, batch=batch),
        grid=(seq // t,),
        in_specs=[pl.BlockSpec((batch, t, d), lambda i: (0, i, 0)),
                  _col_spec(w_in, layer, 2 * width, 0), _col_spec(b_in, layer, 2 * width, 0)]
        + [_full_spec(a.shape) for a in (cw, cb, wg, br, bi, lam, wo)],
        out_specs=pl.BlockSpec((batch, t, d), lambda i: (0, i, 0)),
        out_shape=jax.ShapeDtypeStruct((batch, seq, d), BF16),
        scratch_shapes=[
            pltpu.VMEM((slabs, (CONV_A - 1) * batch + rows, LANES), F32),
            pltpu.VMEM((slabs, rows, LANES), F32),
            pltpu.VMEM((slabs, rows, LANES), F32),
            pltpu.VMEM((slabs, batch, LANES), F32),
        ],
        compiler_params=_seq_params(),
        name="branch_a",
    )(x, w_in, b_in, cw, cb, wg, br, bi, lam, wo)


def _gate_placement():
    p = np.zeros((SPLIT * LANES, LANES), np.float32)
    for hd in range(N_HEADS):
        for e in range(SPLIT):
            p[e * LANES + hd, SPLIT * hd + e] = -1.0
            p[e * LANES + hd, QUERY_LANES + SPLIT * hd + e] = 1.0
    return jnp.asarray(p, BF16)


def _branch_bc_kernel(x_ref, wu_ref, wgate_ref, wgb_ref, wq_ref, wk_ref, wv_ref, wf_ref,
                      bu_ref, bgate_ref, bgb_ref, bq_ref, bk_ref, bv_ref, bf_ref,
                      cw_ref, cb_ref, lng_ref, lnb_ref, wo_ref, place_ref,
                      o_ref, q_ref, k_ref, v_ref, cs_ref,
                      ubuf, vbuf, gbuf, hbuf, wrep, cbuf, cstate, cbefore, *, t, batch, n_tiles):
    i = pl.program_id(0)
    rows = t * batch
    hist = (CONV_B - 1) * batch
    slot_rows = hist + rows
    slabs = ubuf.shape[0]
    width = slabs * LANES
    reps = CONV_CHUNK // SUBLANES
    per_group = MXU_DIM // LANES
    cur = lax.rem(i, 2)
    prev = 1 - cur
    cur0 = pl.multiple_of(cur * slot_rows, SUBLANES)
    prev0 = pl.multiple_of(prev * slot_rows, SUBLANES)

    @pl.when(i == 0)
    def _():
        ubuf[...] = jnp.zeros_like(ubuf)
        gbuf[...] = jnp.zeros_like(gbuf)
        hbuf[...] = jnp.zeros_like(hbuf)
        cstate[...] = jnp.zeros_like(cstate)
        cbefore[...] = jnp.zeros_like(cbefore)
        for j in range(CONV_B):
            wrep[j] = jnp.broadcast_to(cw_ref[j:j + 1, :], (SUBLANES, width))
        wrep[CONV_B] = jnp.broadcast_to(cb_ref[...], (SUBLANES, width))

    o_ref[...] = _dot(hbuf[...], wo_ref[...]).astype(o_ref.dtype).reshape(o_ref.shape)

    xb = x_ref[...].reshape(rows, x_ref.shape[2]).astype(BF16)

    def proj(w, b, g):
        sl = slice(g * MXU_DIM, (g + 1) * MXU_DIM)
        return _dot(xb, w[:, sl]) + b[:, sl]

    def conv_slab(c):
        lanes = _lane_block(c)
        for r0 in range(0, rows, CONV_CHUNK):
            acc = jnp.concatenate([wrep[CONV_B, :, lanes]] * reps, axis=0)
            for j in range(CONV_B):
                acc = acc + jnp.concatenate([wrep[j, :, lanes]] * reps, axis=0) * ubuf[
                    c, pl.ds(prev0 + r0 + j * batch, CONV_CHUNK), :]
            vbuf[c, r0:r0 + CONV_CHUNK, :] = acc

    def fill_slab(c, u):
        for bb in range(batch):
            ubuf[c, pl.ds(cur0 + hist + bb, t, stride=batch), :] = u[bb * t:(bb + 1) * t, :]
        ubuf[c, pl.ds(cur0, hist), :] = ubuf[c, pl.ds(prev0 + rows, hist), :]

    def qkv_group(ref, w, b, g, scale):
        val = proj(w, b, g)
        if scale != 1.0:
            val = val * scale
        ref[:, :, g * MXU_DIM:(g + 1) * MXU_DIM] = val.astype(ref.dtype).reshape(batch, t, MXU_DIM)

    q_scale = LOG2E / math.sqrt(HEAD_DIM)
    for g in range(width // MXU_DIM):
        conv_slab(g * per_group)
        u = proj(wu_ref, bu_ref, g) * _sigmoid(proj(wgate_ref, bgate_ref, g))
        gbuf[cur, :, g * MXU_DIM:(g + 1) * MXU_DIM] = proj(wgb_ref, bgb_ref, g)
        for s in range(per_group):
            fill_slab(g * per_group + s, u[:, _lane_block(s)])
        qkv_group(q_ref, wq_ref, bq_ref, g, q_scale)
        conv_slab(g * per_group + 1)
        qkv_group(k_ref, wk_ref, bk_ref, g, 1.0)
        qkv_group(v_ref, wv_ref, bv_ref, g, 1.0)

    lf = _log_sigmoid(_dot(xb, wf_ref[...]) + bf_ref[...]) * LOG2E
    _store_time_major(cbuf, 0, lf, batch, t)
    start = jnp.where(i < n_tiles, cstate[...], cbefore[...])
    cbefore[...] = start
    c = start
    for tt in range(t):
        c = c + cbuf[0, tt * batch:(tt + 1) * batch, :]
        cbuf[0, tt * batch:(tt + 1) * batch, :] = c
    cstate[...] = c
    rem = _load_batch_major(cbuf, batch, t)
    pieces = []
    for _ in range(SPLIT):
        piece = rem.astype(BF16)
        pieces.append(piece)
        rem = rem - piece.astype(F32)
    cs = _dot(jnp.concatenate(pieces, axis=-1), place_ref[...])
    cs_ref[...] = cs.astype(cs_ref.dtype).reshape(cs_ref.shape)

    y = _silu(_layer_norm(_load_batch_major(vbuf, batch, t), lng_ref[...], lnb_ref[...]))
    hbuf[...] = (y * _silu(gbuf[prev])).astype(hbuf.dtype)


def _branch_bc(x, w_in, b_in, layer, cw, cb, lng, lnb, wo):
    batch, seq, d = x.shape
    width = cw.shape[1]
    assert width == d and (2 + 6) * width % LANES == 0
    f_block = (2 + 6) * width // LANES
    weights = [_col_spec(w_in, layer, width, 2 + j) for j in range(6)]
    weights.append(_col_spec(w_in, layer, LANES, f_block))
    biases = [_col_spec(b_in, layer, width, 2 + j) for j in range(6)]
    biases.append(_col_spec(b_in, layer, LANES, f_block))
    slabs = width // LANES
    t = T_TILE
    rows = t * batch
    n = seq // t
    assert rows >= (CONV_B - 1) * batch and rows % CONV_CHUNK == 0
    place = _gate_placement()
    now = lambda cols: pl.BlockSpec((batch, t, cols), lambda i: (0, jnp.minimum(i, n - 1), 0))
    lag = lambda cols: pl.BlockSpec((batch, t, cols), lambda i: (0, jnp.maximum(i - 2, 0), 0))
    act = lambda cols: jax.ShapeDtypeStruct((batch, seq, cols), BF16)
    return pl.pallas_call(
        functools.partial(_branch_bc_kernel, t=t, batch=batch, n_tiles=n),
        grid=(n + 2,),
        in_specs=[now(d)] + weights + biases
        + [_full_spec(a.shape) for a in (cw, cb, lng, lnb, wo, place)],
        out_specs=[lag(d), now(width), now(width), now(width), now(LANES)],
        out_shape=[act(d), act(width), act(width), act(width), act(LANES)],
        scratch_shapes=[
            pltpu.VMEM((slabs, 2 * ((CONV_B - 1) * batch + rows), LANES), F32),
            pltpu.VMEM((slabs, rows, LANES), F32),
            pltpu.VMEM((2, rows, width), F32),
            pltpu.VMEM((rows, width), BF16),
            pltpu.VMEM((CONV_B + 1, SUBLANES, width), F32),
            pltpu.VMEM((1, rows, LANES), F32),
            pltpu.VMEM((batch, LANES), F32),
            pltpu.VMEM((batch, LANES), F32),
        ],
        compiler_params=_seq_params(),
        name="branch_bc",
    )(x, *([w_in] * 7), *([b_in] * 7), cw, cb, lng, lnb, wo, place)


def _attn_kernel(q_ref, k_ref, v_ref, cs_ref, o_ref, qcat_t, kcat, vm_t, *bufs, tile):
    group = pl.program_id(1)
    seq = q_ref.shape[0]
    pairs = q_ref.shape[1] // LANES
    lane = lax.broadcasted_iota(jnp.int32, (1, LANES), 1)
    sub = lax.broadcasted_iota(jnp.int32, (LANES, 1), 0)
    head_rows = (sub < HEAD_DIM, sub >= HEAD_DIM)
    sum_row = (HEAD_DIM, 0)
    zero = jnp.zeros((), BF16)
    one = jnp.ones((), BF16)
    keep = (lax.broadcasted_iota(jnp.int32, (tile, tile), 0)
            <= lax.broadcasted_iota(jnp.int32, (tile, tile), 1))

    cs = cs_ref[...]
    ct = cs.T
    for pp in range(pairs):
        lanes = _lane_block(pp)
        qt, vt = q_ref[:, lanes].T, v_ref[:, lanes].T
        kcat[pp, :, 0:LANES] = k_ref[:, lanes]
        for h in range(2):
            lo_q = SPLIT * (2 * (pairs * group + pp) + h)
            lo_k = QUERY_LANES + lo_q
            sel_q = (sub >= lo_q) & (sub < lo_q + SPLIT)
            sel_k = (lane >= lo_k) & (lane < lo_k + SPLIT)
            qcat_t[2 * pp + h, 0:LANES, :] = jnp.where(head_rows[h], qt, zero)
            qcat_t[2 * pp + h, LANES:, :] = jnp.where(
                sel_q, one, jnp.where(sub >= QUERY_LANES, ct, zero))
            kcat[pp, :, (h + 1) * LANES:(h + 2) * LANES] = jnp.where(
                sel_k, one, jnp.where(lane < QUERY_LANES, cs, zero))
            vm_t[2 * pp + h] = jnp.where(
                sub == sum_row[h], one, jnp.where(head_rows[h], vt, zero))

    units = [(pp, i, h) for pp in range(pairs) for i in range(seq // tile) for h in range(2)]

    def buffers(unit):
        slot = units.index(unit) % N_BUFFERS
        return bufs[slot], bufs[N_BUFFERS + slot]

    def scores(unit):
        pp, i, h = unit
        q0, klen = i * tile, (i + 1) * tile
        s_buf, _ = buffers(unit)
        keys = jnp.concatenate(
            [kcat[pp, 0:klen, 0:LANES], kcat[pp, 0:klen, (h + 1) * LANES:(h + 2) * LANES]], axis=-1)
        s_buf[0:klen, :] = _dot(keys, qcat_t[2 * pp + h, :, q0:klen])
        s_buf[q0:klen, :] = jnp.where(keep, s_buf[q0:klen, :], MASKED)

    def probs_and_values(unit):
        pp, i, h = unit
        klen = (i + 1) * tile
        s_buf, p_buf = buffers(unit)
        m8 = None
        for c in range(klen // KEY_CHUNK):
            rows = slice(c * KEY_CHUNK, (c + 1) * KEY_CHUNK)
            part = jnp.max(s_buf[rows, :].reshape(KEY_CHUNK // SUBLANES, SUBLANES, tile), axis=0)
            m8 = part if m8 is None else jnp.maximum(m8, part)
        m = jnp.max(m8, axis=0, keepdims=True)
        for c in range(klen // KEY_CHUNK):
            rows = slice(c * KEY_CHUNK, (c + 1) * KEY_CHUNK)
            p_buf[rows, :] = jnp.exp2(s_buf[rows, :] - m).astype(BF16)
        pv = _dot(vm_t[2 * pp + h, :, 0:klen], p_buf[0:klen, :])
        return pv / pv[sum_row[h]:sum_row[h] + 1, :]

    for unit in units[:AHEAD]:
        scores(unit)
    outs = []
    for u, unit in enumerate(units):
        if u + AHEAD < len(units):
            scores(units[u + AHEAD])
        outs.append(probs_and_values(unit))
        pp, i, h = unit
        if h == 1:
            o_ref[i * tile:(i + 1) * tile, _lane_block(pp)] = (
                jnp.where(head_rows[0], outs[0], outs[1]).T.astype(o_ref.dtype))
            outs = []


def _attention(q, k, v, cs):
    batch, seq, width = q.shape
    tile = Q_TILE
    assert tile % KEY_CHUNK == 0 and AHEAD < N_BUFFERS
    assert QUERY_LANES + SPLIT * N_HEADS <= LANES and SPLIT * N_HEADS <= QUERY_LANES
    pairs = PAIRS_PER_STEP
    block = pl.BlockSpec((None, seq, pairs * LANES), lambda b, g: (b, 0, g))
    return pl.pallas_call(
        functools.partial(_attn_kernel, tile=tile),
        grid=(batch, width // (pairs * LANES)),
        in_specs=[block, block, block, pl.BlockSpec((None, seq, LANES), lambda b, g: (b, 0, 0))],
        out_specs=block,
        out_shape=jax.ShapeDtypeStruct((batch, seq, width), BF16),
        scratch_shapes=[
            pltpu.VMEM((2 * pairs, 2 * LANES, seq), BF16),
            pltpu.VMEM((pairs, seq, 3 * LANES), BF16),
            pltpu.VMEM((2 * pairs, LANES, seq), BF16),
        ] + [pltpu.VMEM((seq, tile), F32)] * N_BUFFERS + [pltpu.VMEM((seq, tile), BF16)] * N_BUFFERS,
        compiler_params=pltpu.CompilerParams(
            dimension_semantics=("arbitrary", "arbitrary"), vmem_limit_bytes=VMEM_LIMIT),
        name="fox_attention",
    )(q, k, v, cs)


def _merge_kernel(x_ref, ya_ref, yb_ref, oc_ref, wg_ref, bg_ref, woc_ref, wo_ref, bo_ref,
                  lng_ref, lnb_ref, out_ref, *, alpha):
    d = x_ref.shape[1]
    x = x_ref[...]
    pg = _dot(x.astype(BF16), wg_ref[...]) + bg_ref[...]
    hc = oc_ref[...].astype(F32) * _silu(pg[:, :d])
    yc = _dot(hc.astype(BF16), woc_ref[...])
    mixed = (_sigmoid(pg[:, d:2 * d]) * ya_ref[...].astype(F32)
             + _sigmoid(pg[:, 2 * d:3 * d]) * yb_ref[...].astype(F32)
             + _sigmoid(pg[:, 3 * d:]) * yc)
    out = _dot(mixed.astype(BF16), wo_ref[...]) + bo_ref[...]
    out_ref[...] = _layer_norm(alpha * x + out, lng_ref[...], lnb_ref[...]).astype(out_ref.dtype)


def _merge(x, ya, yb, oc, wg, bg, woc, wo, bo, lng, lnb, *, alpha):
    batch, seq, d = x.shape
    tile = MERGE_TILE
    rows = pl.BlockSpec((None, tile, d), lambda b, i: (b, i, 0))
    return pl.pallas_call(
        functools.partial(_merge_kernel, alpha=alpha),
        grid=(batch, seq // tile),
        in_specs=[rows, rows, rows, rows]
        + [_full_spec(a.shape) for a in (wg, bg, woc, wo, bo, lng, lnb)],
        out_specs=rows,
        out_shape=jax.ShapeDtypeStruct((batch, seq, d), F32),
        compiler_params=pltpu.CompilerParams(
            dimension_semantics=("arbitrary", "arbitrary"), vmem_limit_bytes=VMEM_LIMIT),
        name="merge_out",
    )(x, ya, yb, oc, wg, bg, woc, wo, bo, lng, lnb)


def _block_diag_gates(wr, wi):
    per_tile = MXU_DIM // wr.shape[1]
    eye = jnp.eye(per_tile, dtype=wr.dtype)

    def bd(w):
        w4 = w.reshape(-1, per_tile, w.shape[1], w.shape[2])
        return jnp.einsum("gaij,ab->gaibj", w4, eye).reshape(-1, MXU_DIM, MXU_DIM)

    return jnp.concatenate([bd(wr), bd(wi)], axis=-1).astype(BF16)


def kernel(x, w_in, b_in, conv_a_w, conv_a_b, lru_wr, lru_br, lru_wi, lru_bi, lru_lambda,
           w_out_a, conv_b_w, conv_b_b, lnb_g, lnb_b, w_out_b, w_out_c, w_o, b_o, ln_g, ln_b):
    batch, seq, d = x.shape
    depth = w_in.shape[0]
    assert batch == SUBLANES and d == N_HEADS * HEAD_DIM
    alpha = float((2 * depth) ** 0.25)
    row = lambda a: a.reshape(1, -1)

    o_gc = 8 * d + N_HEADS
    w_bf = w_in.astype(BF16)
    b_3d = b_in.reshape(depth, 1, -1)

    for l in range(depth):
        ya = _branch_a(x, w_bf, b_3d, l, conv_a_w[l], row(conv_a_b[l]),
                       _block_diag_gates(lru_wr[l], lru_wi[l]), row(lru_br[l]), row(lru_bi[l]),
                       row(lru_lambda[l]), w_out_a[l].astype(BF16))
        yb, q, k, v, cs = _branch_bc(x, w_bf, b_3d, l, conv_b_w[l], row(conv_b_b[l]),
                                     row(lnb_g[l]), row(lnb_b[l]), w_out_b[l].astype(BF16))
        oc = _attention(q, k, v, cs)

        x = _merge(x, ya, yb, oc, w_bf[l, :, o_gc:], row(b_in[l, o_gc:]),
                   w_out_c[l].astype(BF16), w_o[l].astype(BF16), row(b_o[l]),
                   row(ln_g[l]), row(ln_b[l]), alpha=alpha)
    return x
```

```python
import functools
import math

import jax
import jax.numpy as jnp
import numpy as np
from jax import lax
from jax.experimental import pallas as pl
from jax.experimental.pallas import tpu as pltpu

F32 = jnp.float32
BF16 = jnp.bfloat16

LRU_C = 8.0
LN_EPS = 1e-5
CONV_A = 4
CONV_B = 31
N_HEADS = 16
HEAD_DIM = 64
LANES = 128
SUBLANES = 8
MXU_DIM = 256
VMEM_LIMIT = 56 * 1024 * 1024
LOG2E = math.log2(math.e)
MASKED = -1e30
SPLIT = 3
QUERY_LANES = 64

T_TILE = 64
CONV_CHUNK = 64
Q_TILE = 256
KEY_CHUNK = 128
AHEAD = 3
N_BUFFERS = 4
PAIRS_PER_STEP = 2
A_T_TILE = 128
STAGE3_AFTER_GROUP = 1
MERGE_TILE = 512


def _sigmoid(x):
    return 0.5 * jnp.tanh(0.5 * x) + 0.5


def _silu(x):
    return x * _sigmoid(x)


def _log_sigmoid(x):
    return jnp.minimum(x, 0.0) - jnp.log1p(jnp.exp(-jnp.abs(x)))


def _layer_norm(x, g, b):
    mu = jnp.mean(x, axis=-1, keepdims=True)
    xc = x - mu
    var = jnp.mean(xc * xc, axis=-1, keepdims=True)
    return xc * lax.rsqrt(var + LN_EPS) * g + b


def _dot(a, b):
    return jnp.dot(a, b, preferred_element_type=F32)


def _lane_block(c):
    return slice(c * LANES, (c + 1) * LANES)


def _store_time_major(dst, row0, val, batch, t):
    for c in range(dst.shape[0]):
        for b in range(batch):
            dst[c, pl.ds(row0 + b, t, stride=batch), :] = val[b * t:(b + 1) * t, _lane_block(c)]


def _load_batch_major(src, batch, t):
    return jnp.concatenate(
        [jnp.concatenate([src[c, pl.ds(b, t, stride=batch), :] for c in range(src.shape[0])], axis=-1)
         for b in range(batch)], axis=0)


def _full_spec(shape):
    return pl.BlockSpec(shape, lambda *_: (0,) * len(shape))


def _col_spec(arr, layer, cols, block):
    return pl.BlockSpec((None, arr.shape[1], cols), lambda *_: (layer, 0, block))


def _seq_params():
    return pltpu.CompilerParams(dimension_semantics=("arbitrary",), vmem_limit_bytes=VMEM_LIMIT)


def _branch_a_kernel(x_ref, w_ref, b_ref, cw_ref, cb_ref, wg_ref, br_ref, bi_ref, lam_ref,
                     wo_ref, o_ref, xbuf, a_buf, h_buf, hstate, *, t, batch):
    rows = t * batch
    hist = (CONV_A - 1) * batch
    slabs = xbuf.shape[0]
    width = slabs * LANES

    @pl.when(pl.program_id(0) == 0)
    def _():
        xbuf[:, 0:hist, :] = jnp.zeros((slabs, hist, LANES), F32)
        hstate[...] = jnp.zeros_like(hstate)

    xb = x_ref[...].reshape(rows, x_ref.shape[2]).astype(BF16)
    p = _dot(xb, w_ref[...]) + b_ref[...]
    ga = p[:, width:]
    _store_time_major(xbuf, hist, p[:, :width], batch, t)

    ys = []
    for c in range(slabs):
        lanes = _lane_block(c)
        y = cb_ref[:, lanes] + cw_ref[0:1, lanes] * xbuf[c, 0:rows, :]
        for j in range(1, CONV_A):
            y = y + cw_ref[j:j + 1, lanes] * xbuf[c, j * batch:j * batch + rows, :]
        xbuf[c, 0:hist, :] = xbuf[c, rows:rows + hist, :]
        ys.append(y)

    log_sig = _log_sigmoid(lam_ref[...])
    per_tile = MXU_DIM // LANES
    for g in range(width // MXU_DIM):
        yb = jnp.concatenate(ys[g * per_tile:(g + 1) * per_tile], axis=-1).astype(BF16)
        gates = _dot(yb, wg_ref[g])
        for s in range(per_tile):
            c = g * per_tile + s
            lanes = _lane_block(c)
            r = _sigmoid(gates[:, _lane_block(s)] + br_ref[:, lanes])
            ig = _sigmoid(gates[:, _lane_block(per_tile + s)] + bi_ref[:, lanes])
            log_a = (LRU_C * r) * log_sig[:, lanes]
            a = jnp.exp(log_a)
            gain = jnp.sqrt(-jnp.tanh(log_a) * (1.0 + a * a))
            a_buf[c] = a
            h_buf[c] = gain * (ig * ys[c])

    def step(tt, hs):
        r0 = pl.multiple_of(tt * batch, batch)
        out = []
        for c in range(slabs):
            h = a_buf[c, pl.ds(r0, batch), :] * hs[c] + h_buf[c, pl.ds(r0, batch), :]
            h_buf[c, pl.ds(r0, batch), :] = h
            out.append(h)
        return tuple(out)

    hs = lax.fori_loop(0, t, step, tuple(hstate[c] for c in range(slabs)), unroll=8)
    for c in range(slabs):
        hstate[c] = hs[c]

    ha = _load_batch_major(h_buf, batch, t) * _silu(ga)
    o_ref[...] = _dot(ha.astype(BF16), wo_ref[...]).astype(o_ref.dtype).reshape(o_ref.shape)


def _branch_a(x, w_in, b_in, layer, cw, cb, wg, br, bi, lam, wo):
    batch, seq, d = x.shape
    width = cw.shape[1]
    slabs = width // LANES
    t = A_T_TILE
    rows = t * batch
    return pl.pallas_call(
        functools.partial(_branch_a_kernel, t=t, batch=batch),
        grid=(seq // t,),
        in_specs=[pl.BlockSpec((batch, t, d), lambda i: (0, i, 0)),
                  _col_spec(w_in, layer, 2 * width, 0), _col_spec(b_in, layer, 2 * width, 0)]
        + [_full_spec(a.shape) for a in (cw, cb, wg, br, bi, lam, wo)],
        out_specs=pl.BlockSpec((batch, t, d), lambda i: (0, i, 0)),
        out_shape=jax.ShapeDtypeStruct((batch, seq, d), BF16),
        scratch_shapes=[
            pltpu.VMEM((slabs, (CONV_A - 1) * batch + rows, LANES), F32),
            pltpu.VMEM((slabs, rows, LANES), F32),
            pltpu.VMEM((slabs, rows, LANES), F32),
            pltpu.VMEM((slabs, batch, LANES), F32),
        ],
        compiler_params=_seq_params(),
        name="branch_a",
    )(x, w_in, b_in, cw, cb, wg, br, bi, lam, wo)


def _gate_placement():
    p = np.zeros((SPLIT * LANES, LANES), np.float32)
    for hd in range(N_HEADS):
        for e in range(SPLIT):
            p[e * LANES + hd, SPLIT * hd + e] = -1.0
            p[e * LANES + hd, QUERY_LANES + SPLIT * hd + e] = 1.0
    return jnp.asarray(p, BF16)


def _branch_bc_kernel(x_ref, wu_ref, wgate_ref, wgb_ref, wq_ref, wk_ref, wv_ref, wf_ref,
                      bu_ref, bgate_ref, bgb_ref, bq_ref, bk_ref, bv_ref, bf_ref,
                      cw_ref, cb_ref, lng_ref, lnb_ref, wo_ref, place_ref,
                      o_ref, q_ref, k_ref, v_ref, cs_ref,
                      ubuf, vbuf, gbuf, hbuf, wrep, cbuf, cstate, cbefore, *, t, batch, n_tiles):
    i = pl.program_id(0)
    rows = t * batch
    hist = (CONV_B - 1) * batch
    slot_rows = hist + rows
    slabs = ubuf.shape[0]
    width = slabs * LANES
    reps = CONV_CHUNK // SUBLANES
    per_group = MXU_DIM // LANES
    cur = lax.rem(i, 2)
    prev = 1 - cur
    cur0 = pl.multiple_of(cur * slot_rows, SUBLANES)
    prev0 = pl.multiple_of(prev * slot_rows, SUBLANES)

    @pl.when(i == 0)
    def _():
        ubuf[...] = jnp.zeros_like(ubuf)
        gbuf[...] = jnp.zeros_like(gbuf)
        hbuf[...] = jnp.zeros_like(hbuf)
        cstate[...] = jnp.zeros_like(cstate)
        cbefore[...] = jnp.zeros_like(cbefore)
        for j in range(CONV_B):
            wrep[j] = jnp.broadcast_to(cw_ref[j:j + 1, :], (SUBLANES, width))
        wrep[CONV_B] = jnp.broadcast_to(cb_ref[...], (SUBLANES, width))

    xb = x_ref[...].reshape(rows, x_ref.shape[2]).astype(BF16)

    def proj(w, b, g):
        sl = slice(g * MXU_DIM, (g + 1) * MXU_DIM)
        return _dot(xb, w[:, sl]) + b[:, sl]

    def conv_slab(c):
        lanes = _lane_block(c)
        for r0 in range(0, rows, CONV_CHUNK):
            acc = jnp.concatenate([wrep[CONV_B, :, lanes]] * reps, axis=0)
            for j in range(CONV_B):
                acc = acc + jnp.concatenate([wrep[j, :, lanes]] * reps, axis=0) * ubuf[
                    c, pl.ds(prev0 + r0 + j * batch, CONV_CHUNK), :]
            vbuf[c, r0:r0 + CONV_CHUNK, :] = acc

    def fill_slab(c, u):
        for bb in range(batch):
            ubuf[c, pl.ds(cur0 + hist + bb, t, stride=batch), :] = u[bb * t:(bb + 1) * t, :]
        ubuf[c, pl.ds(cur0, hist), :] = ubuf[c, pl.ds(prev0 + rows, hist), :]

    def qkv_group(ref, w, b, g, scale):
        val = proj(w, b, g)
        if scale != 1.0:
            val = val * scale
        ref[:, :, g * MXU_DIM:(g + 1) * MXU_DIM] = val.astype(ref.dtype).reshape(batch, t, MXU_DIM)

    q_scale = LOG2E / math.sqrt(HEAD_DIM)
    for g in range(width // MXU_DIM):
        conv_slab(g * per_group)
        u = proj(wu_ref, bu_ref, g) * (1.0 + jnp.tanh(proj(wgate_ref, bgate_ref, g)))
        gbuf[cur, :, g * MXU_DIM:(g + 1) * MXU_DIM] = proj(wgb_ref, bgb_ref, g)
        for s in range(per_group):
            fill_slab(g * per_group + s, u[:, _lane_block(s)])
        if g == STAGE3_AFTER_GROUP:
            o_ref[...] = _dot(hbuf[...], wo_ref[...]).astype(o_ref.dtype).reshape(o_ref.shape)
        qkv_group(q_ref, wq_ref, bq_ref, g, q_scale)
        conv_slab(g * per_group + 1)
        qkv_group(k_ref, wk_ref, bk_ref, g, 1.0)
        qkv_group(v_ref, wv_ref, bv_ref, g, 1.0)

    lf = _log_sigmoid(_dot(xb, wf_ref[...]) + bf_ref[...]) * LOG2E
    _store_time_major(cbuf, 0, lf, batch, t)
    start = jnp.where(i < n_tiles, cstate[...], cbefore[...])
    cbefore[...] = start
    c = start
    for tt in range(t):
        c = c + cbuf[0, tt * batch:(tt + 1) * batch, :]
        cbuf[0, tt * batch:(tt + 1) * batch, :] = c
    cstate[...] = c
    rem = _load_batch_major(cbuf, batch, t)
    pieces = []
    for _ in range(SPLIT):
        piece = rem.astype(BF16)
        pieces.append(piece)
        rem = rem - piece.astype(F32)
    cs = _dot(jnp.concatenate(pieces, axis=-1), place_ref[...])
    cs_ref[...] = cs.astype(cs_ref.dtype).reshape(cs_ref.shape)

    yh = _layer_norm(_load_batch_major(vbuf, batch, t), 0.5 * lng_ref[...], 0.5 * lnb_ref[...])
    gh = gbuf[prev]
    hbuf[...] = ((yh * (1.0 + jnp.tanh(yh))) * (gh * (1.0 + jnp.tanh(gh)))).astype(hbuf.dtype)


def _branch_bc(x, w_in, b_in, layer, cw, cb, lng, lnb, wo):
    batch, seq, d = x.shape
    width = cw.shape[1]
    assert width == d and (2 + 6) * width % LANES == 0
    f_block = (2 + 6) * width // LANES
    weights = [_col_spec(w_in, layer, width, 2 + j) for j in range(6)]
    weights.append(_col_spec(w_in, layer, LANES, f_block))
    biases = [_col_spec(b_in, layer, width, 2 + j) for j in range(6)]
    biases.append(_col_spec(b_in, layer, LANES, f_block))
    slabs = width // LANES
    t = T_TILE
    rows = t * batch
    n = seq // t
    assert rows >= (CONV_B - 1) * batch and rows % CONV_CHUNK == 0
    place = _gate_placement()
    now = lambda cols: pl.BlockSpec((batch, t, cols), lambda i: (0, jnp.minimum(i, n - 1), 0))
    lag = lambda cols: pl.BlockSpec((batch, t, cols), lambda i: (0, jnp.maximum(i - 2, 0), 0))
    act = lambda cols: jax.ShapeDtypeStruct((batch, seq, cols), BF16)
    return pl.pallas_call(
        functools.partial(_branch_bc_kernel, t=t, batch=batch, n_tiles=n),
        grid=(n + 2,),
        in_specs=[now(d)] + weights + biases
        + [_full_spec(a.shape) for a in (cw, cb, lng, lnb, wo, place)],
        out_specs=[lag(d), now(width), now(width), now(width), now(LANES)],
        out_shape=[act(d), act(width), act(width), act(width), act(LANES)],
        scratch_shapes=[
            pltpu.VMEM((slabs, 2 * ((CONV_B - 1) * batch + rows), LANES), F32),
            pltpu.VMEM((slabs, rows, LANES), F32),
            pltpu.VMEM((2, rows, width), F32),
            pltpu.VMEM((rows, width), BF16),
            pltpu.VMEM((CONV_B + 1, SUBLANES, width), F32),
            pltpu.VMEM((1, rows, LANES), F32),
            pltpu.VMEM((batch, LANES), F32),
            pltpu.VMEM((batch, LANES), F32),
        ],
        compiler_params=_seq_params(),
        name="branch_bc",
    )(x, *([w_in] * 7), *([b_in] * 7), cw, cb, lng, lnb, wo, place)


def _attn_kernel(q_ref, k_ref, v_ref, cs_ref, o_ref, qcat_t, kcat, vm_t, *bufs, tile):
    group = pl.program_id(1)
    seq = q_ref.shape[0]
    pairs = q_ref.shape[1] // LANES
    lane = lax.broadcasted_iota(jnp.int32, (1, LANES), 1)
    sub = lax.broadcasted_iota(jnp.int32, (LANES, 1), 0)
    head_rows = (sub < HEAD_DIM, sub >= HEAD_DIM)
    sum_row = (HEAD_DIM, 0)
    zero = jnp.zeros((), BF16)
    one = jnp.ones((), BF16)
    keep = (lax.broadcasted_iota(jnp.int32, (tile, tile), 0)
            <= lax.broadcasted_iota(jnp.int32, (tile, tile), 1))

    cs = cs_ref[...]
    ct = cs.T
    for pp in range(pairs):
        lanes = _lane_block(pp)
        qt, vt = q_ref[:, lanes].T, v_ref[:, lanes].T
        kcat[pp, :, 0:LANES] = k_ref[:, lanes]
        for h in range(2):
            lo_q = SPLIT * (2 * (pairs * group + pp) + h)
            lo_k = QUERY_LANES + lo_q
            sel_q = (sub >= lo_q) & (sub < lo_q + SPLIT)
            sel_k = (lane >= lo_k) & (lane < lo_k + SPLIT)
            qcat_t[2 * pp + h, 0:LANES, :] = jnp.where(head_rows[h], qt, zero)
            qcat_t[2 * pp + h, LANES:, :] = jnp.where(
                sel_q, one, jnp.where(sub >= QUERY_LANES, ct, zero))
            kcat[pp, :, (h + 1) * LANES:(h + 2) * LANES] = jnp.where(
                sel_k, one, jnp.where(lane < QUERY_LANES, cs, zero))
            vm_t[2 * pp + h] = jnp.where(
                sub == sum_row[h], one, jnp.where(head_rows[h], vt, zero))

    units = [(pp, i, h) for pp in range(pairs) for i in range(seq // tile) for h in range(2)]

    def buffers(unit):
        slot = units.index(unit) % N_BUFFERS
        return bufs[slot], bufs[N_BUFFERS + slot]

    def scores(unit):
        pp, i, h = unit
        q0, klen = i * tile, (i + 1) * tile
        s_buf, _ = buffers(unit)
        keys = jnp.concatenate(
            [kcat[pp, 0:klen, 0:LANES], kcat[pp, 0:klen, (h + 1) * LANES:(h + 2) * LANES]], axis=-1)
        s_buf[0:klen, :] = _dot(keys, qcat_t[2 * pp + h, :, q0:klen])
        s_buf[q0:klen, :] = jnp.where(keep, s_buf[q0:klen, :], MASKED)

    def probs_and_values(unit):
        pp, i, h = unit
        klen = (i + 1) * tile
        s_buf, p_buf = buffers(unit)
        m8 = None
        for c in range(klen // KEY_CHUNK):
            rows = slice(c * KEY_CHUNK, (c + 1) * KEY_CHUNK)
            part = jnp.max(s_buf[rows, :].reshape(KEY_CHUNK // SUBLANES, SUBLANES, tile), axis=0)
            m8 = part if m8 is None else jnp.maximum(m8, part)
        m = jnp.max(m8, axis=0, keepdims=True)
        for c in range(klen // KEY_CHUNK):
            rows = slice(c * KEY_CHUNK, (c + 1) * KEY_CHUNK)
            p_buf[rows, :] = jnp.exp2(s_buf[rows, :] - m).astype(BF16)
        pv = _dot(vm_t[2 * pp + h, :, 0:klen], p_buf[0:klen, :])
        return pv / pv[sum_row[h]:sum_row[h] + 1, :]

    for unit in units[:AHEAD]:
        scores(unit)
    outs = []
    for u, unit in enumerate(units):
        if u + AHEAD < len(units):
            scores(units[u + AHEAD])
        outs.append(probs_and_values(unit))
        pp, i, h = unit
        if h == 1:
            o_ref[i * tile:(i + 1) * tile, _lane_block(pp)] = (
                jnp.where(head_rows[0], outs[0], outs[1]).T.astype(o_ref.dtype))
            outs = []


def _attention(q, k, v, cs):
    batch, seq, width = q.shape
    tile = Q_TILE
    assert tile % KEY_CHUNK == 0 and AHEAD < N_BUFFERS
    assert QUERY_LANES + SPLIT * N_HEADS <= LANES and SPLIT * N_HEADS <= QUERY_LANES
    pairs = PAIRS_PER_STEP
    block = pl.BlockSpec((None, seq, pairs * LANES), lambda b, g: (b, 0, g))
    return pl.pallas_call(
        functools.partial(_attn_kernel, tile=tile),
        grid=(batch, width // (pairs * LANES)),
        in_specs=[block, block, block, pl.BlockSpec((None, seq, LANES), lambda b, g: (b, 0, 0))],
        out_specs=block,
        out_shape=jax.ShapeDtypeStruct((batch, seq, width), BF16),
        scratch_shapes=[
            pltpu.VMEM((2 * pairs, 2 * LANES, seq), BF16),
            pltpu.VMEM((pairs, seq, 3 * LANES), BF16),
            pltpu.VMEM((2 * pairs, LANES, seq), BF16),
        ] + [pltpu.VMEM((seq, tile), F32)] * N_BUFFERS + [pltpu.VMEM((seq, tile), BF16)] * N_BUFFERS,
        compiler_params=pltpu.CompilerParams(
            dimension_semantics=("arbitrary", "arbitrary"), vmem_limit_bytes=VMEM_LIMIT),
        name="fox_attention",
    )(q, k, v, cs)


def _merge_kernel(x_ref, ya_ref, yb_ref, oc_ref, wg_ref, bg_ref, woc_ref, wo_ref, bo_ref,
                  lng_ref, lnb_ref, out_ref, *, alpha):
    d = x_ref.shape[1]
    x = x_ref[...]
    pg = _dot(x.astype(BF16), wg_ref[...]) + bg_ref[...]
    hc = oc_ref[...].astype(F32) * _silu(pg[:, :d])
    yc = _dot(hc.astype(BF16), woc_ref[...])
    mixed = (_sigmoid(pg[:, d:2 * d]) * ya_ref[...].astype(F32)
             + _sigmoid(pg[:, 2 * d:3 * d]) * yb_ref[...].astype(F32)
             + _sigmoid(pg[:, 3 * d:]) * yc)
    out = _dot(mixed.astype(BF16), wo_ref[...]) + bo_ref[...]
    out_ref[...] = _layer_norm(alpha * x + out, lng_ref[...], lnb_ref[...]).astype(out_ref.dtype)


def _merge(x, ya, yb, oc, wg, bg, woc, wo, bo, lng, lnb, *, alpha):
    batch, seq, d = x.shape
    tile = MERGE_TILE
    rows = pl.BlockSpec((None, tile, d), lambda b, i: (b, i, 0))
    return pl.pallas_call(
        functools.partial(_merge_kernel, alpha=alpha),
        grid=(batch, seq // tile),
        in_specs=[rows, rows, rows, rows]
        + [_full_spec(a.shape) for a in (wg, bg, woc, wo, bo, lng, lnb)],
        out_specs=rows,
        out_shape=jax.ShapeDtypeStruct((batch, seq, d), F32),
        compiler_params=pltpu.CompilerParams(
            dimension_semantics=("arbitrary", "arbitrary"), vmem_limit_bytes=VMEM_LIMIT),
        name="merge_out",
    )(x, ya, yb, oc, wg, bg, woc, wo, bo, lng, lnb)


def _block_diag_gates(wr, wi):
    per_tile = MXU_DIM // wr.shape[1]
    eye = jnp.eye(per_tile, dtype=wr.dtype)

    def bd(w):
        w4 = w.reshape(-1, per_tile, w.shape[1], w.shape[2])
        return jnp.einsum("gaij,ab->gaibj", w4, eye).reshape(-1, MXU_DIM, MXU_DIM)

    return jnp.concatenate([bd(wr), bd(wi)], axis=-1).astype(BF16)


def kernel(x, w_in, b_in, conv_a_w, conv_a_b, lru_wr, lru_br, lru_wi, lru_bi, lru_lambda,
           w_out_a, conv_b_w, conv_b_b, lnb_g, lnb_b, w_out_b, w_out_c, w_o, b_o, ln_g, ln_b):
    batch, seq, d = x.shape
    depth = w_in.shape[0]
    assert batch == SUBLANES and d == N_HEADS * HEAD_DIM
    alpha = float((2 * depth) ** 0.25)
    row = lambda a: a.reshape(1, -1)

    o_gc = 8 * d + N_HEADS
    col_scale = jnp.ones((w_in.shape[2],), F32).at[2 * d:5 * d].set(0.5)
    w_bf = (w_in * col_scale).astype(BF16)
    b_3d = (b_in * col_scale).reshape(depth, 1, -1)

    for l in range(depth):
        ya = _branch_a(x, w_bf, b_3d, l, conv_a_w[l], row(conv_a_b[l]),
                       _block_diag_gates(lru_wr[l], lru_wi[l]), row(lru_br[l]), row(lru_bi[l]),
                       row(lru_lambda[l]), w_out_a[l].astype(BF16))
        yb, q, k, v, cs = _branch_bc(x, w_bf, b_3d, l, conv_b_w[l], row(conv_b_b[l]),
                                     row(lnb_g[l]), row(lnb_b[l]), w_out_b[l].astype(BF16))
        oc = _attention(q, k, v, cs)

        x = _merge(x, ya, yb, oc, w_bf[l, :, o_gc:], row(b_in[l, o_gc:]),
                   w_out_c[l].astype(BF16), w_o[l].astype(BF16), row(b_o[l]),
                   row(ln_g[l]), row(ln_b[l]), alpha=alpha)
    return x
```

```python
import functools
import math

import jax
import jax.numpy as jnp
import numpy as np
from jax import lax
from jax.experimental import pallas as pl
from jax.experimental.pallas import tpu as pltpu

F32 = jnp.float32
BF16 = jnp.bfloat16

LRU_C = 8.0
LN_EPS = 1e-5
CONV_A = 4
CONV_B = 31
N_HEADS = 16
HEAD_DIM = 64
LANES = 128
SUBLANES = 8
MXU_DIM = 256
VMEM_LIMIT = 56 * 1024 * 1024
LOG2E = math.log2(math.e)
MASKED = -1e30
SPLIT = 3
QUERY_LANES = 64

T_TILE = 64
CONV_CHUNK = 64
Q_TILE = 256
KEY_CHUNK = 128
AHEAD = 3
N_BUFFERS = 4
PAIRS_PER_STEP = 2
A_T_TILE = 128
STAGE3_AFTER_GROUP = 1
MERGE_TILE = 512


def _sigmoid(x):
    return 0.5 * jnp.tanh(0.5 * x) + 0.5


def _silu(x):
    return x * _sigmoid(x)


def _log_sigmoid(x):
    return jnp.minimum(x, 0.0) - jnp.log1p(jnp.exp(-jnp.abs(x)))


def _layer_norm(x, g, b):
    mu = jnp.mean(x, axis=-1, keepdims=True)
    xc = x - mu
    var = jnp.mean(xc * xc, axis=-1, keepdims=True)
    return xc * lax.rsqrt(var + LN_EPS) * g + b


def _dot(a, b):
    return jnp.dot(a, b, preferred_element_type=F32)


def _lane_block(c):
    return slice(c * LANES, (c + 1) * LANES)


def _store_time_major(dst, row0, val, batch, t):
    for c in range(dst.shape[0]):
        for b in range(batch):
            dst[c, pl.ds(row0 + b, t, stride=batch), :] = val[b * t:(b + 1) * t, _lane_block(c)]


def _load_batch_major(src, batch, t):
    return jnp.concatenate(
        [jnp.concatenate([src[c, pl.ds(b, t, stride=batch), :] for c in range(src.shape[0])], axis=-1)
         for b in range(batch)], axis=0)


def _full_spec(shape):
    return pl.BlockSpec(shape, lambda *_: (0,) * len(shape))


def _col_spec(arr, layer, cols, block):
    return pl.BlockSpec((None, arr.shape[1], cols), lambda *_: (layer, 0, block))


def _seq_params():
    return pltpu.CompilerParams(dimension_semantics=("arbitrary",), vmem_limit_bytes=VMEM_LIMIT)


def _branch_a_kernel(x_ref, w_ref, b_ref, cw_ref, cb_ref, wg_ref, br_ref, bi_ref, lam_ref,
                     wo_ref, o_ref, xbuf, a_buf, h_buf, hstate, *, t, batch):
    rows = t * batch
    hist = (CONV_A - 1) * batch
    slabs = xbuf.shape[0]
    width = slabs * LANES

    @pl.when(pl.program_id(0) == 0)
    def _():
        xbuf[:, 0:hist, :] = jnp.zeros((slabs, hist, LANES), F32)
        hstate[...] = jnp.zeros_like(hstate)

    xb = x_ref[...].reshape(rows, x_ref.shape[2]).astype(BF16)
    p = _dot(xb, w_ref[...]) + b_ref[...]
    ga = p[:, width:]
    _store_time_major(xbuf, hist, p[:, :width], batch, t)

    ys = []
    for c in range(slabs):
        lanes = _lane_block(c)
        y = cb_ref[:, lanes] + cw_ref[0:1, lanes] * xbuf[c, 0:rows, :]
        for j in range(1, CONV_A):
            y = y + cw_ref[j:j + 1, lanes] * xbuf[c, j * batch:j * batch + rows, :]
        xbuf[c, 0:hist, :] = xbuf[c, rows:rows + hist, :]
        ys.append(y)

    log_sig = _log_sigmoid(lam_ref[...])
    per_tile = MXU_DIM // LANES
    for g in range(width // MXU_DIM):
        yb = jnp.concatenate(ys[g * per_tile:(g + 1) * per_tile], axis=-1).astype(BF16)
        gates = _dot(yb, wg_ref[g])
        for s in range(per_tile):
            c = g * per_tile + s
            lanes = _lane_block(c)
            r = _sigmoid(gates[:, _lane_block(s)] + br_ref[:, lanes])
            ig = _sigmoid(gates[:, _lane_block(per_tile + s)] + bi_ref[:, lanes])
            log_a = (LRU_C * r) * log_sig[:, lanes]
            a = jnp.exp(log_a)
            gain = jnp.sqrt(-jnp.tanh(log_a) * (1.0 + a * a))
            a_buf[c] = a
            h_buf[c] = gain * (ig * ys[c])

    def step(tt, hs):
        r0 = pl.multiple_of(tt * batch, batch)
        out = []
        for c in range(slabs):
            h = a_buf[c, pl.ds(r0, batch), :] * hs[c] + h_buf[c, pl.ds(r0, batch), :]
            h_buf[c, pl.ds(r0, batch), :] = h
            out.append(h)
        return tuple(out)

    hs = lax.fori_loop(0, t, step, tuple(hstate[c] for c in range(slabs)), unroll=8)
    for c in range(slabs):
        hstate[c] = hs[c]

    ha = _load_batch_major(h_buf, batch, t) * _silu(ga)
    o_ref[...] = _dot(ha.astype(BF16), wo_ref[...]).astype(o_ref.dtype).reshape(o_ref.shape)


def _branch_a(x, w_in, b_in, layer, cw, cb, wg, br, bi, lam, wo):
    batch, seq, d = x.shape
    width = cw.shape[1]
    slabs = width // LANES
    t = A_T_TILE
    rows = t * batch
    return pl.pallas_call(
        functools.partial(_branch_a_kernel, t=t, batch=batch),
        grid=(seq // t,),
        in_specs=[pl.BlockSpec((batch, t, d), lambda i: (0, i, 0)),
                  _col_spec(w_in, layer, 2 * width, 0), _col_spec(b_in, layer, 2 * width, 0)]
        + [_full_spec(a.shape) for a in (cw, cb, wg, br, bi, lam, wo)],
        out_specs=pl.BlockSpec((batch, t, d), lambda i: (0, i, 0)),
        out_shape=jax.ShapeDtypeStruct((batch, seq, d), BF16),
        scratch_shapes=[
            pltpu.VMEM((slabs, (CONV_A - 1) * batch + rows, LANES), F32),
            pltpu.VMEM((slabs, rows, LANES), F32),
            pltpu.VMEM((slabs, rows, LANES), F32),
            pltpu.VMEM((slabs, batch, LANES), F32),
        ],
        compiler_params=_seq_params(),
        name="branch_a",
    )(x, w_in, b_in, cw, cb, wg, br, bi, lam, wo)


def _gate_placement():
    p = np.zeros((SPLIT * LANES, LANES), np.float32)
    for hd in range(N_HEADS):
        for e in range(SPLIT):
            p[e * LANES + hd, SPLIT * hd + e] = -1.0
            p[e * LANES + hd, QUERY_LANES + SPLIT * hd + e] = 1.0
    return jnp.asarray(p, BF16)


def _branch_bc_kernel(x_ref, wu_ref, wgate_ref, wgb_ref, wq_ref, wk_ref, wv_ref, wf_ref,
                      bu_ref, bgate_ref, bgb_ref, bq_ref, bk_ref, bv_ref, bf_ref,
                      cw_ref, cb_ref, lng_ref, lnb_ref, wo_ref, place_ref,
                      o_ref, q_ref, k_ref, v_ref, cs_ref,
                      ubuf, vbuf, gbuf, hbuf, wrep, cbuf, cstate, cbefore, *, t, batch, n_tiles):
    i = pl.program_id(0)
    rows = t * batch
    hist = (CONV_B - 1) * batch
    slot_rows = hist + rows
    slabs = ubuf.shape[0]
    width = slabs * LANES
    reps = CONV_CHUNK // SUBLANES
    per_group = MXU_DIM // LANES
    cur = lax.rem(i, 2)
    prev = 1 - cur
    cur0 = pl.multiple_of(cur * slot_rows, SUBLANES)
    prev0 = pl.multiple_of(prev * slot_rows, SUBLANES)

    @pl.when(i == 0)
    def _():
        ubuf[...] = jnp.zeros_like(ubuf)
        gbuf[...] = jnp.zeros_like(gbuf)
        hbuf[...] = jnp.zeros_like(hbuf)
        cstate[...] = jnp.zeros_like(cstate)
        cbefore[...] = jnp.zeros_like(cbefore)
        for j in range(CONV_B):
            wrep[j] = jnp.broadcast_to(cw_ref[j:j + 1, :], (SUBLANES, width))
        wrep[CONV_B] = jnp.broadcast_to(cb_ref[...], (SUBLANES, width))

    xb = x_ref[...].reshape(rows, x_ref.shape[2]).astype(BF16)

    def proj(w, b, g):
        sl = slice(g * MXU_DIM, (g + 1) * MXU_DIM)
        return _dot(xb, w[:, sl]) + b[:, sl]

    def conv_slab(c):
        lanes = _lane_block(c)
        for r0 in range(0, rows, CONV_CHUNK):
            acc = jnp.concatenate([wrep[CONV_B, :, lanes]] * reps, axis=0)
            for j in range(CONV_B):
                acc = acc + jnp.concatenate([wrep[j, :, lanes]] * reps, axis=0) * ubuf[
                    c, pl.ds(prev0 + r0 + j * batch, CONV_CHUNK), :]
            vbuf[c, r0:r0 + CONV_CHUNK, :] = acc

    def fill_slab(c, u):
        for bb in range(batch):
            ubuf[c, pl.ds(cur0 + hist + bb, t, stride=batch), :] = u[bb * t:(bb + 1) * t, :]
        ubuf[c, pl.ds(cur0, hist), :] = ubuf[c, pl.ds(prev0 + rows, hist), :]

    def qkv_group(ref, w, b, g, scale):
        val = proj(w, b, g)
        if scale != 1.0:
            val = val * scale
        ref[:, :, g * MXU_DIM:(g + 1) * MXU_DIM] = val.astype(ref.dtype).reshape(batch, t, MXU_DIM)

    q_scale = LOG2E / math.sqrt(HEAD_DIM)
    for g in range(width // MXU_DIM):
        conv_slab(g * per_group)
        u = proj(wu_ref, bu_ref, g) * _sigmoid(proj(wgate_ref, bgate_ref, g))
        gbuf[cur, :, g * MXU_DIM:(g + 1) * MXU_DIM] = 0.5 * proj(wgb_ref, bgb_ref, g)
        for s in range(per_group):
            fill_slab(g * per_group + s, u[:, _lane_block(s)])
        if g == STAGE3_AFTER_GROUP:
            o_ref[...] = _dot(hbuf[...], wo_ref[...]).astype(o_ref.dtype).reshape(o_ref.shape)
        qkv_group(q_ref, wq_ref, bq_ref, g, q_scale)
        conv_slab(g * per_group + 1)
        qkv_group(k_ref, wk_ref, bk_ref, g, 1.0)
        qkv_group(v_ref, wv_ref, bv_ref, g, 1.0)

    lf = _log_sigmoid(_dot(xb, wf_ref[...]) + bf_ref[...]) * LOG2E
    _store_time_major(cbuf, 0, lf, batch, t)
    start = jnp.where(i < n_tiles, cstate[...], cbefore[...])
    cbefore[...] = start
    c = start
    for tt in range(t):
        c = c + cbuf[0, tt * batch:(tt + 1) * batch, :]
        cbuf[0, tt * batch:(tt + 1) * batch, :] = c
    cstate[...] = c
    rem = _load_batch_major(cbuf, batch, t)
    pieces = []
    for _ in range(SPLIT):
        piece = rem.astype(BF16)
        pieces.append(piece)
        rem = rem - piece.astype(F32)
    cs = _dot(jnp.concatenate(pieces, axis=-1), place_ref[...])
    cs_ref[...] = cs.astype(cs_ref.dtype).reshape(cs_ref.shape)

    yh = _layer_norm(_load_batch_major(vbuf, batch, t), 0.5 * lng_ref[...], 0.5 * lnb_ref[...])
    gh = gbuf[prev]
    hbuf[...] = ((yh * (1.0 + jnp.tanh(yh))) * (gh * (1.0 + jnp.tanh(gh)))).astype(hbuf.dtype)


def _branch_bc(x, w_in, b_in, layer, cw, cb, lng, lnb, wo):
    batch, seq, d = x.shape
    width = cw.shape[1]
    assert width == d and (2 + 6) * width % LANES == 0
    f_block = (2 + 6) * width // LANES
    weights = [_col_spec(w_in, layer, width, 2 + j) for j in range(6)]
    weights.append(_col_spec(w_in, layer, LANES, f_block))
    biases = [_col_spec(b_in, layer, width, 2 + j) for j in range(6)]
    biases.append(_col_spec(b_in, layer, LANES, f_block))
    slabs = width // LANES
    t = T_TILE
    rows = t * batch
    n = seq // t
    assert rows >= (CONV_B - 1) * batch and rows % CONV_CHUNK == 0
    place = _gate_placement()
    now = lambda cols: pl.BlockSpec((batch, t, cols), lambda i: (0, jnp.minimum(i, n - 1), 0))
    lag = lambda cols: pl.BlockSpec((batch, t, cols), lambda i: (0, jnp.maximum(i - 2, 0), 0))
    act = lambda cols: jax.ShapeDtypeStruct((batch, seq, cols), BF16)
    return pl.pallas_call(
        functools.partial(_branch_bc_kernel, t=t, batch=batch, n_tiles=n),
        grid=(n + 2,),
        in_specs=[now(d)] + weights + biases
        + [_full_spec(a.shape) for a in (cw, cb, lng, lnb, wo, place)],
        out_specs=[lag(d), now(width), now(width), now(width), now(LANES)],
        out_shape=[act(d), act(width), act(width), act(width), act(LANES)],
        scratch_shapes=[
            pltpu.VMEM((slabs, 2 * ((CONV_B - 1) * batch + rows), LANES), F32),
            pltpu.VMEM((slabs, rows, LANES), F32),
            pltpu.VMEM((2, rows, width), F32),
            pltpu.VMEM((rows, width), BF16),
            pltpu.VMEM((CONV_B + 1, SUBLANES, width), F32),
            pltpu.VMEM((1, rows, LANES), F32),
            pltpu.VMEM((batch, LANES), F32),
            pltpu.VMEM((batch, LANES), F32),
        ],
        compiler_params=_seq_params(),
        name="branch_bc",
    )(x, *([w_in] * 7), *([b_in] * 7), cw, cb, lng, lnb, wo, place)


def _attn_kernel(q_ref, k_ref, v_ref, cs_ref, o_ref, qcat_t, kcat, vm_t, *bufs, tile):
    group = pl.program_id(1)
    seq = q_ref.shape[0]
    pairs = q_ref.shape[1] // LANES
    lane = lax.broadcasted_iota(jnp.int32, (1, LANES), 1)
    sub = lax.broadcasted_iota(jnp.int32, (LANES, 1), 0)
    head_rows = (sub < HEAD_DIM, sub >= HEAD_DIM)
    sum_row = (HEAD_DIM, 0)
    zero = jnp.zeros((), BF16)
    one = jnp.ones((), BF16)
    keep = (lax.broadcasted_iota(jnp.int32, (tile, tile), 0)
            <= lax.broadcasted_iota(jnp.int32, (tile, tile), 1))

    cs = cs_ref[...]
    ct = cs.T
    for pp in range(pairs):
        lanes = _lane_block(pp)
        qt, vt = q_ref[:, lanes].T, v_ref[:, lanes].T
        kcat[pp, :, 0:LANES] = k_ref[:, lanes]
        for h in range(2):
            lo_q = SPLIT * (2 * (pairs * group + pp) + h)
            lo_k = QUERY_LANES + lo_q
            sel_q = (sub >= lo_q) & (sub < lo_q + SPLIT)
            sel_k = (lane >= lo_k) & (lane < lo_k + SPLIT)
            qcat_t[2 * pp + h, 0:LANES, :] = jnp.where(head_rows[h], qt, zero)
            qcat_t[2 * pp + h, LANES:, :] = jnp.where(
                sel_q, one, jnp.where(sub >= QUERY_LANES, ct, zero))
            kcat[pp, :, (h + 1) * LANES:(h + 2) * LANES] = jnp.where(
                sel_k, one, jnp.where(lane < QUERY_LANES, cs, zero))
            vm_t[2 * pp + h] = jnp.where(
                sub == sum_row[h], one, jnp.where(head_rows[h], vt, zero))

    units = [(pp, i, h) for pp in range(pairs) for i in range(seq // tile) for h in range(2)]

    def buffers(unit):
        slot = units.index(unit) % N_BUFFERS
        return bufs[slot], bufs[N_BUFFERS + slot]

    def scores(unit):
        pp, i, h = unit
        q0, klen = i * tile, (i + 1) * tile
        s_buf, _ = buffers(unit)
        keys = jnp.concatenate(
            [kcat[pp, 0:klen, 0:LANES], kcat[pp, 0:klen, (h + 1) * LANES:(h + 2) * LANES]], axis=-1)
        s_buf[0:klen, :] = _dot(keys, qcat_t[2 * pp + h, :, q0:klen])
        s_buf[q0:klen, :] = jnp.where(keep, s_buf[q0:klen, :], MASKED)

    def probs_and_values(unit):
        pp, i, h = unit
        klen = (i + 1) * tile
        s_buf, p_buf = buffers(unit)
        m8 = None
        for c in range(klen // KEY_CHUNK):
            rows = slice(c * KEY_CHUNK, (c + 1) * KEY_CHUNK)
            part = jnp.max(s_buf[rows, :].reshape(KEY_CHUNK // SUBLANES, SUBLANES, tile), axis=0)
            m8 = part if m8 is None else jnp.maximum(m8, part)
        m = jnp.max(m8, axis=0, keepdims=True)
        for c in range(klen // KEY_CHUNK):
            rows = slice(c * KEY_CHUNK, (c + 1) * KEY_CHUNK)
            p_buf[rows, :] = jnp.exp2(s_buf[rows, :] - m).astype(BF16)
        pv = _dot(vm_t[2 * pp + h, :, 0:klen], p_buf[0:klen, :])
        return pv / pv[sum_row[h]:sum_row[h] + 1, :]

    for unit in units[:AHEAD]:
        scores(unit)
    outs = []
    for u, unit in enumerate(units):
        if u + AHEAD < len(units):
            scores(units[u + AHEAD])
        outs.append(probs_and_values(unit))
        pp, i, h = unit
        if h == 1:
            o_ref[i * tile:(i + 1) * tile, _lane_block(pp)] = (
                jnp.where(head_rows[0], outs[0], outs[1]).T.astype(o_ref.dtype))
            outs = []


def _attention(q, k, v, cs):
    batch, seq, width = q.shape
    tile = Q_TILE
    assert tile % KEY_CHUNK == 0 and AHEAD < N_BUFFERS
    assert QUERY_LANES + SPLIT * N_HEADS <= LANES and SPLIT * N_HEADS <= QUERY_LANES
    pairs = PAIRS_PER_STEP
    block = pl.BlockSpec((None, seq, pairs * LANES), lambda b, g: (b, 0, g))
    return pl.pallas_call(
        functools.partial(_attn_kernel, tile=tile),
        grid=(batch, width // (pairs * LANES)),
        in_specs=[block, block, block, pl.BlockSpec((None, seq, LANES), lambda b, g: (b, 0, 0))],
        out_specs=block,
        out_shape=jax.ShapeDtypeStruct((batch, seq, width), BF16),
        scratch_shapes=[
            pltpu.VMEM((2 * pairs, 2 * LANES, seq), BF16),
            pltpu.VMEM((pairs, seq, 3 * LANES), BF16),
            pltpu.VMEM((2 * pairs, LANES, seq), BF16),
        ] + [pltpu.VMEM((seq, tile), F32)] * N_BUFFERS + [pltpu.VMEM((seq, tile), BF16)] * N_BUFFERS,
        compiler_params=pltpu.CompilerParams(
            dimension_semantics=("arbitrary", "arbitrary"), vmem_limit_bytes=VMEM_LIMIT),
        name="fox_attention",
    )(q, k, v, cs)


def _merge_kernel(x_ref, ya_ref, yb_ref, oc_ref, wg_ref, bg_ref, woc_ref, wo_ref, bo_ref,
                  lng_ref, lnb_ref, out_ref, *, alpha):
    d = x_ref.shape[1]
    x = x_ref[...]
    pg = _dot(x.astype(BF16), wg_ref[...]) + bg_ref[...]
    hc = oc_ref[...].astype(F32) * _silu(pg[:, :d])
    yc = _dot(hc.astype(BF16), woc_ref[...])
    mixed = (_sigmoid(pg[:, d:2 * d]) * ya_ref[...].astype(F32)
             + _sigmoid(pg[:, 2 * d:3 * d]) * yb_ref[...].astype(F32)
             + _sigmoid(pg[:, 3 * d:]) * yc)
    out = _dot(mixed.astype(BF16), wo_ref[...]) + bo_ref[...]
    out_ref[...] = _layer_norm(alpha * x + out, lng_ref[...], lnb_ref[...]).astype(out_ref.dtype)


def _merge(x, ya, yb, oc, wg, bg, woc, wo, bo, lng, lnb, *, alpha):
    batch, seq, d = x.shape
    tile = MERGE_TILE
    rows = pl.BlockSpec((None, tile, d), lambda b, i: (b, i, 0))
    return pl.pallas_call(
        functools.partial(_merge_kernel, alpha=alpha),
        grid=(batch, seq // tile),
        in_specs=[rows, rows, rows, rows]
        + [_full_spec(a.shape) for a in (wg, bg, woc, wo, bo, lng, lnb)],
        out_specs=rows,
        out_shape=jax.ShapeDtypeStruct((batch, seq, d), F32),
        compiler_params=pltpu.CompilerParams(
            dimension_semantics=("arbitrary", "arbitrary"), vmem_limit_bytes=VMEM_LIMIT),
        name="merge_out",
    )(x, ya, yb, oc, wg, bg, woc, wo, bo, lng, lnb)


def _block_diag_gates(wr, wi):
    per_tile = MXU_DIM // wr.shape[1]
    eye = jnp.eye(per_tile, dtype=wr.dtype)

    def bd(w):
        w4 = w.reshape(-1, per_tile, w.shape[1], w.shape[2])
        return jnp.einsum("gaij,ab->gaibj", w4, eye).reshape(-1, MXU_DIM, MXU_DIM)

    return jnp.concatenate([bd(wr), bd(wi)], axis=-1).astype(BF16)


def kernel(x, w_in, b_in, conv_a_w, conv_a_b, lru_wr, lru_br, lru_wi, lru_bi, lru_lambda,
           w_out_a, conv_b_w, conv_b_b, lnb_g, lnb_b, w_out_b, w_out_c, w_o, b_o, ln_g, ln_b):
    batch, seq, d = x.shape
    depth = w_in.shape[0]
    assert batch == SUBLANES and d == N_HEADS * HEAD_DIM
    alpha = float((2 * depth) ** 0.25)
    row = lambda a: a.reshape(1, -1)

    o_gc = 8 * d + N_HEADS
    w_bf = w_in.astype(BF16)
    b_3d = b_in.reshape(depth, 1, -1)

    for l in range(depth):
        ya = _branch_a(x, w_bf, b_3d, l, conv_a_w[l], row(conv_a_b[l]),
                       _block_diag_gates(lru_wr[l], lru_wi[l]), row(lru_br[l]), row(lru_bi[l]),
                       row(lru_lambda[l]), w_out_a[l].astype(BF16))
        yb, q, k, v, cs = _branch_bc(x, w_bf, b_3d, l, conv_b_w[l], row(conv_b_b[l]),
                                     row(lnb_g[l]), row(lnb_b[l]), w_out_b[l].astype(BF16))
        oc = _attention(q, k, v, cs)

        x = _merge(x, ya, yb, oc, w_bf[l, :, o_gc:], row(b_in[l, o_gc:]),
                   w_out_c[l].astype(BF16), w_o[l].astype(BF16), row(b_o[l]),
                   row(ln_g[l]), row(ln_b[l]), alpha=alpha)
    return x
```

```python
import functools
import math

import jax
import jax.numpy as jnp
import numpy as np
from jax import lax
from jax.experimental import pallas as pl
from jax.experimental.pallas import tpu as pltpu

F32 = jnp.float32
BF16 = jnp.bfloat16

LRU_C = 8.0
LN_EPS = 1e-5
CONV_A = 4
CONV_B = 31
N_HEADS = 16
HEAD_DIM = 64
LANES = 128
SUBLANES = 8
MXU_DIM = 256
VMEM_LIMIT = 56 * 1024 * 1024
LOG2E = math.log2(math.e)
MASKED = -1e30
SPLIT = 3
QUERY_LANES = 64

T_TILE = 64
CONV_CHUNK = 64
Q_TILE = 256
KEY_CHUNK = 128
AHEAD = 5
N_BUFFERS = 6
PAIRS_PER_STEP = 2
A_T_TILE = 128
STAGE3_AFTER_GROUP = 1
MERGE_TILE = 512


def _sigmoid(x):
    return 0.5 * jnp.tanh(0.5 * x) + 0.5


def _silu(x):
    return x * _sigmoid(x)


def _log_sigmoid(x):
    return jnp.minimum(x, 0.0) - jnp.log1p(jnp.exp(-jnp.abs(x)))


def _layer_norm(x, g, b):
    mu = jnp.mean(x, axis=-1, keepdims=True)
    xc = x - mu
    var = jnp.mean(xc * xc, axis=-1, keepdims=True)
    return xc * lax.rsqrt(var + LN_EPS) * g + b


def _dot(a, b):
    return jnp.dot(a, b, preferred_element_type=F32)


def _lane_block(c):
    return slice(c * LANES, (c + 1) * LANES)


def _store_time_major(dst, row0, val, batch, t):
    for c in range(dst.shape[0]):
        for b in range(batch):
            dst[c, pl.ds(row0 + b, t, stride=batch), :] = val[b * t:(b + 1) * t, _lane_block(c)]


def _load_batch_major(src, batch, t):
    return jnp.concatenate(
        [jnp.concatenate([src[c, pl.ds(b, t, stride=batch), :] for c in range(src.shape[0])], axis=-1)
         for b in range(batch)], axis=0)


def _full_spec(shape):
    return pl.BlockSpec(shape, lambda *_: (0,) * len(shape))


def _col_spec(arr, layer, cols, block):
    return pl.BlockSpec((None, arr.shape[1], cols), lambda *_: (layer, 0, block))


def _seq_params():
    return pltpu.CompilerParams(dimension_semantics=("arbitrary",), vmem_limit_bytes=VMEM_LIMIT)


def _branch_a_kernel(x_ref, w_ref, b_ref, cw_ref, cb_ref, wg_ref, br_ref, bi_ref, lam_ref,
                     wo_ref, o_ref, xbuf, a_buf, h_buf, hstate, *, t, batch):
    rows = t * batch
    hist = (CONV_A - 1) * batch
    slabs = xbuf.shape[0]
    width = slabs * LANES

    @pl.when(pl.program_id(0) == 0)
    def _():
        xbuf[:, 0:hist, :] = jnp.zeros((slabs, hist, LANES), F32)
        hstate[...] = jnp.zeros_like(hstate)

    xb = x_ref[...].reshape(rows, x_ref.shape[2]).astype(BF16)
    p = _dot(xb, w_ref[...]) + b_ref[...]
    ga = p[:, width:]
    _store_time_major(xbuf, hist, p[:, :width], batch, t)

    ys = []
    for c in range(slabs):
        lanes = _lane_block(c)
        y = cb_ref[:, lanes] + cw_ref[0:1, lanes] * xbuf[c, 0:rows, :]
        for j in range(1, CONV_A):
            y = y + cw_ref[j:j + 1, lanes] * xbuf[c, j * batch:j * batch + rows, :]
        xbuf[c, 0:hist, :] = xbuf[c, rows:rows + hist, :]
        ys.append(y)

    log_sig = _log_sigmoid(lam_ref[...])
    per_tile = MXU_DIM // LANES
    for g in range(width // MXU_DIM):
        yb = jnp.concatenate(ys[g * per_tile:(g + 1) * per_tile], axis=-1).astype(BF16)
        gates = _dot(yb, wg_ref[g])
        for s in range(per_tile):
            c = g * per_tile + s
            lanes = _lane_block(c)
            r = _sigmoid(gates[:, _lane_block(s)] + br_ref[:, lanes])
            ig = _sigmoid(gates[:, _lane_block(per_tile + s)] + bi_ref[:, lanes])
            log_a = (LRU_C * r) * log_sig[:, lanes]
            a = jnp.exp(log_a)
            gain = jnp.sqrt(-jnp.tanh(log_a) * (1.0 + a * a))
            a_buf[c] = a
            h_buf[c] = gain * (ig * ys[c])

    def step(tt, hs):
        r0 = pl.multiple_of(tt * batch, batch)
        out = []
        for c in range(slabs):
            h = a_buf[c, pl.ds(r0, batch), :] * hs[c] + h_buf[c, pl.ds(r0, batch), :]
            h_buf[c, pl.ds(r0, batch), :] = h
            out.append(h)
        return tuple(out)

    hs = lax.fori_loop(0, t, step, tuple(hstate[c] for c in range(slabs)), unroll=8)
    for c in range(slabs):
        hstate[c] = hs[c]

    ha = _load_batch_major(h_buf, batch, t) * _silu(ga)
    o_ref[...] = _dot(ha.astype(BF16), wo_ref[...]).astype(o_ref.dtype).reshape(o_ref.shape)


def _branch_a(x, w_in, b_in, layer, cw, cb, wg, br, bi, lam, wo):
    batch, seq, d = x.shape
    width = cw.shape[1]
    slabs = width // LANES
    t = A_T_TILE
    rows = t * batch
    return pl.pallas_call(
        functools.partial(_branch_a_kernel, t=t, batch=batch),
        grid=(seq // t,),
        in_specs=[pl.BlockSpec((batch, t, d), lambda i: (0, i, 0)),
                  _col_spec(w_in, layer, 2 * width, 0), _col_spec(b_in, layer, 2 * width, 0)]
        + [_full_spec(a.shape) for a in (cw, cb, wg, br, bi, lam, wo)],
        out_specs=pl.BlockSpec((batch, t, d), lambda i: (0, i, 0)),
        out_shape=jax.ShapeDtypeStruct((batch, seq, d), BF16),
        scratch_shapes=[
            pltpu.VMEM((slabs, (CONV_A - 1) * batch + rows, LANES), F32),
            pltpu.VMEM((slabs, rows, LANES), F32),
            pltpu.VMEM((slabs, rows, LANES), F32),
            pltpu.VMEM((slabs, batch, LANES), F32),
        ],
        compiler_params=_seq_params(),
        name="branch_a",
    )(x, w_in, b_in, cw, cb, wg, br, bi, lam, wo)


def _gate_placement():
    p = np.zeros((SPLIT * LANES, LANES), np.float32)
    for hd in range(N_HEADS):
        for e in range(SPLIT):
            p[e * LANES + hd, SPLIT * hd + e] = -1.0
            p[e * LANES + hd, QUERY_LANES + SPLIT * hd + e] = 1.0
    return jnp.asarray(p, BF16)


def _branch_bc_kernel(x_ref, wu_ref, wgate_ref, wgb_ref, wq_ref, wk_ref, wv_ref, wf_ref,
                      bu_ref, bgate_ref, bgb_ref, bq_ref, bk_ref, bv_ref, bf_ref,
                      cw_ref, cb_ref, lng_ref, lnb_ref, wo_ref, place_ref,
                      o_ref, q_ref, k_ref, v_ref, cs_ref,
                      ubuf, vbuf, gbuf, hbuf, wrep, cbuf, cstate, cbefore, *, t, batch, n_tiles):
    i = pl.program_id(0)
    rows = t * batch
    hist = (CONV_B - 1) * batch
    slot_rows = hist + rows
    slabs = ubuf.shape[0]
    width = slabs * LANES
    reps = CONV_CHUNK // SUBLANES
    per_group = MXU_DIM // LANES
    cur = lax.rem(i, 2)
    prev = 1 - cur
    cur0 = pl.multiple_of(cur * slot_rows, SUBLANES)
    prev0 = pl.multiple_of(prev * slot_rows, SUBLANES)

    @pl.when(i == 0)
    def _():
        ubuf[...] = jnp.zeros_like(ubuf)
        gbuf[...] = jnp.zeros_like(gbuf)
        hbuf[...] = jnp.zeros_like(hbuf)
        cstate[...] = jnp.zeros_like(cstate)
        cbefore[...] = jnp.zeros_like(cbefore)
        for j in range(CONV_B):
            wrep[j] = jnp.broadcast_to(cw_ref[j:j + 1, :], (SUBLANES, width))
        wrep[CONV_B] = jnp.broadcast_to(cb_ref[...], (SUBLANES, width))

    xb = x_ref[...].reshape(rows, x_ref.shape[2]).astype(BF16)

    def proj(w, b, g):
        sl = slice(g * MXU_DIM, (g + 1) * MXU_DIM)
        return _dot(xb, w[:, sl]) + b[:, sl]

    def conv_slab(c):
        lanes = _lane_block(c)
        for r0 in range(0, rows, CONV_CHUNK):
            acc = jnp.concatenate([wrep[CONV_B, :, lanes]] * reps, axis=0)
            for j in range(CONV_B):
                acc = acc + jnp.concatenate([wrep[j, :, lanes]] * reps, axis=0) * ubuf[
                    c, pl.ds(prev0 + r0 + j * batch, CONV_CHUNK), :]
            vbuf[c, r0:r0 + CONV_CHUNK, :] = acc

    def fill_slab(c, u):
        for bb in range(batch):
            ubuf[c, pl.ds(cur0 + hist + bb, t, stride=batch), :] = u[bb * t:(bb + 1) * t, :]
        ubuf[c, pl.ds(cur0, hist), :] = ubuf[c, pl.ds(prev0 + rows, hist), :]

    def qkv_group(ref, w, b, g, scale):
        val = proj(w, b, g)
        if scale != 1.0:
            val = val * scale
        ref[:, :, g * MXU_DIM:(g + 1) * MXU_DIM] = val.astype(ref.dtype).reshape(batch, t, MXU_DIM)

    q_scale = LOG2E / math.sqrt(HEAD_DIM)
    for g in range(width // MXU_DIM):
        conv_slab(g * per_group)
        u = proj(wu_ref, bu_ref, g) * _sigmoid(proj(wgate_ref, bgate_ref, g))
        gbuf[cur, :, g * MXU_DIM:(g + 1) * MXU_DIM] = 0.5 * proj(wgb_ref, bgb_ref, g)
        for s in range(per_group):
            fill_slab(g * per_group + s, u[:, _lane_block(s)])
        if g == STAGE3_AFTER_GROUP:
            o_ref[...] = _dot(hbuf[...], wo_ref[...]).astype(o_ref.dtype).reshape(o_ref.shape)
        qkv_group(q_ref, wq_ref, bq_ref, g, q_scale)
        conv_slab(g * per_group + 1)
        qkv_group(k_ref, wk_ref, bk_ref, g, 1.0)
        qkv_group(v_ref, wv_ref, bv_ref, g, 1.0)

    lf = _log_sigmoid(_dot(xb, wf_ref[...]) + bf_ref[...]) * LOG2E
    _store_time_major(cbuf, 0, lf, batch, t)
    start = jnp.where(i < n_tiles, cstate[...], cbefore[...])
    cbefore[...] = start
    c = start
    for tt in range(t):
        c = c + cbuf[0, tt * batch:(tt + 1) * batch, :]
        cbuf[0, tt * batch:(tt + 1) * batch, :] = c
    cstate[...] = c
    rem = _load_batch_major(cbuf, batch, t)
    pieces = []
    for _ in range(SPLIT):
        piece = rem.astype(BF16)
        pieces.append(piece)
        rem = rem - piece.astype(F32)
    cs = _dot(jnp.concatenate(pieces, axis=-1), place_ref[...])
    cs_ref[...] = cs.astype(cs_ref.dtype).reshape(cs_ref.shape)

    yh = _layer_norm(_load_batch_major(vbuf, batch, t), 0.5 * lng_ref[...], 0.5 * lnb_ref[...])
    gh = gbuf[prev]
    hbuf[...] = ((yh * (1.0 + jnp.tanh(yh))) * (gh * (1.0 + jnp.tanh(gh)))).astype(hbuf.dtype)


def _branch_bc(x, w_in, b_in, layer, cw, cb, lng, lnb, wo):
    batch, seq, d = x.shape
    width = cw.shape[1]
    assert width == d and (2 + 6) * width % LANES == 0
    f_block = (2 + 6) * width // LANES
    weights = [_col_spec(w_in, layer, width, 2 + j) for j in range(6)]
    weights.append(_col_spec(w_in, layer, LANES, f_block))
    biases = [_col_spec(b_in, layer, width, 2 + j) for j in range(6)]
    biases.append(_col_spec(b_in, layer, LANES, f_block))
    slabs = width // LANES
    t = T_TILE
    rows = t * batch
    n = seq // t
    assert rows >= (CONV_B - 1) * batch and rows % CONV_CHUNK == 0
    place = _gate_placement()
    now = lambda cols: pl.BlockSpec((batch, t, cols), lambda i: (0, jnp.minimum(i, n - 1), 0))
    lag = lambda cols: pl.BlockSpec((batch, t, cols), lambda i: (0, jnp.maximum(i - 2, 0), 0))
    act = lambda cols: jax.ShapeDtypeStruct((batch, seq, cols), BF16)
    return pl.pallas_call(
        functools.partial(_branch_bc_kernel, t=t, batch=batch, n_tiles=n),
        grid=(n + 2,),
        in_specs=[now(d)] + weights + biases
        + [_full_spec(a.shape) for a in (cw, cb, lng, lnb, wo, place)],
        out_specs=[lag(d), now(width), now(width), now(width), now(LANES)],
        out_shape=[act(d), act(width), act(width), act(width), act(LANES)],
        scratch_shapes=[
            pltpu.VMEM((slabs, 2 * ((CONV_B - 1) * batch + rows), LANES), F32),
            pltpu.VMEM((slabs, rows, LANES), F32),
            pltpu.VMEM((2, rows, width), F32),
            pltpu.VMEM((rows, width), BF16),
            pltpu.VMEM((CONV_B + 1, SUBLANES, width), F32),
            pltpu.VMEM((1, rows, LANES), F32),
            pltpu.VMEM((batch, LANES), F32),
            pltpu.VMEM((batch, LANES), F32),
        ],
        compiler_params=_seq_params(),
        name="branch_bc",
    )(x, *([w_in] * 7), *([b_in] * 7), cw, cb, lng, lnb, wo, place)


def _attn_kernel(q_ref, k_ref, v_ref, cs_ref, o_ref, qcat_t, kcat, vm_t, *bufs, tile):
    group = pl.program_id(1)
    seq = q_ref.shape[0]
    pairs = q_ref.shape[1] // LANES
    lane = lax.broadcasted_iota(jnp.int32, (1, LANES), 1)
    sub = lax.broadcasted_iota(jnp.int32, (LANES, 1), 0)
    head_rows = (sub < HEAD_DIM, sub >= HEAD_DIM)
    sum_row = (HEAD_DIM, 0)
    zero = jnp.zeros((), BF16)
    one = jnp.ones((), BF16)
    keep = (lax.broadcasted_iota(jnp.int32, (tile, tile), 0)
            <= lax.broadcasted_iota(jnp.int32, (tile, tile), 1))

    cs = cs_ref[...]
    ct = cs.T
    for pp in range(pairs):
        lanes = _lane_block(pp)
        qt, vt = q_ref[:, lanes].T, v_ref[:, lanes].T
        kcat[pp, :, 0:LANES] = k_ref[:, lanes]
        for h in range(2):
            lo_q = SPLIT * (2 * (pairs * group + pp) + h)
            lo_k = QUERY_LANES + lo_q
            sel_q = (sub >= lo_q) & (sub < lo_q + SPLIT)
            sel_k = (lane >= lo_k) & (lane < lo_k + SPLIT)
            qcat_t[2 * pp + h, 0:LANES, :] = jnp.where(head_rows[h], qt, zero)
            qcat_t[2 * pp + h, LANES:, :] = jnp.where(
                sel_q, one, jnp.where(sub >= QUERY_LANES, ct, zero))
            kcat[pp, :, (h + 1) * LANES:(h + 2) * LANES] = jnp.where(
                sel_k, one, jnp.where(lane < QUERY_LANES, cs, zero))
            vm_t[2 * pp + h] = jnp.where(
                sub == sum_row[h], one, jnp.where(head_rows[h], vt, zero))

    units = [(pp, i, h) for pp in range(pairs) for i in range(seq // tile) for h in range(2)]

    def buffers(unit):
        slot = units.index(unit) % N_BUFFERS
        return bufs[slot], bufs[N_BUFFERS + slot]

    def scores(unit):
        pp, i, h = unit
        q0, klen = i * tile, (i + 1) * tile
        s_buf, _ = buffers(unit)
        keys = jnp.concatenate(
            [kcat[pp, 0:klen, 0:LANES], kcat[pp, 0:klen, (h + 1) * LANES:(h + 2) * LANES]], axis=-1)
        s_buf[0:klen, :] = _dot(keys, qcat_t[2 * pp + h, :, q0:klen])
        s_buf[q0:klen, :] = jnp.where(keep, s_buf[q0:klen, :], MASKED)

    def probs_and_values(unit):
        pp, i, h = unit
        klen = (i + 1) * tile
        s_buf, p_buf = buffers(unit)
        m8 = None
        for c in range(klen // KEY_CHUNK):
            rows = slice(c * KEY_CHUNK, (c + 1) * KEY_CHUNK)
            part = jnp.max(s_buf[rows, :].reshape(KEY_CHUNK // SUBLANES, SUBLANES, tile), axis=0)
            m8 = part if m8 is None else jnp.maximum(m8, part)
        m = jnp.max(m8, axis=0, keepdims=True)
        for c in range(klen // KEY_CHUNK):
            rows = slice(c * KEY_CHUNK, (c + 1) * KEY_CHUNK)
            p_buf[rows, :] = jnp.exp2(s_buf[rows, :] - m).astype(BF16)
        pv = _dot(vm_t[2 * pp + h, :, 0:klen], p_buf[0:klen, :])
        return pv / pv[sum_row[h]:sum_row[h] + 1, :]

    for unit in units[:AHEAD]:
        scores(unit)
    outs = []
    for u, unit in enumerate(units):
        if u + AHEAD < len(units):
            scores(units[u + AHEAD])
        outs.append(probs_and_values(unit))
        pp, i, h = unit
        if h == 1:
            o_ref[i * tile:(i + 1) * tile, _lane_block(pp)] = (
                jnp.where(head_rows[0], outs[0], outs[1]).T.astype(o_ref.dtype))
            outs = []


def _attention(q, k, v, cs):
    batch, seq, width = q.shape
    tile = Q_TILE
    assert tile % KEY_CHUNK == 0 and AHEAD < N_BUFFERS
    assert QUERY_LANES + SPLIT * N_HEADS <= LANES and SPLIT * N_HEADS <= QUERY_LANES
    pairs = PAIRS_PER_STEP
    block = pl.BlockSpec((None, seq, pairs * LANES), lambda b, g: (b, 0, g))
    return pl.pallas_call(
        functools.partial(_attn_kernel, tile=tile),
        grid=(batch, width // (pairs * LANES)),
        in_specs=[block, block, block, pl.BlockSpec((None, seq, LANES), lambda b, g: (b, 0, 0))],
        out_specs=block,
        out_shape=jax.ShapeDtypeStruct((batch, seq, width), BF16),
        scratch_shapes=[
            pltpu.VMEM((2 * pairs, 2 * LANES, seq), BF16),
            pltpu.VMEM((pairs, seq, 3 * LANES), BF16),
            pltpu.VMEM((2 * pairs, LANES, seq), BF16),
        ] + [pltpu.VMEM((seq, tile), F32)] * N_BUFFERS + [pltpu.VMEM((seq, tile), BF16)] * N_BUFFERS,
        compiler_params=pltpu.CompilerParams(
            dimension_semantics=("arbitrary", "arbitrary"), vmem_limit_bytes=VMEM_LIMIT),
        name="fox_attention",
    )(q, k, v, cs)


def _merge_kernel(x_ref, ya_ref, yb_ref, oc_ref, wg_ref, bg_ref, woc_ref, wo_ref, bo_ref,
                  lng_ref, lnb_ref, out_ref, *, alpha):
    d = x_ref.shape[1]
    x = x_ref[...]
    pg = _dot(x.astype(BF16), wg_ref[...]) + bg_ref[...]
    hc = oc_ref[...].astype(F32) * _silu(pg[:, :d])
    yc = _dot(hc.astype(BF16), woc_ref[...])
    mixed = (_sigmoid(pg[:, d:2 * d]) * ya_ref[...].astype(F32)
             + _sigmoid(pg[:, 2 * d:3 * d]) * yb_ref[...].astype(F32)
             + _sigmoid(pg[:, 3 * d:]) * yc)
    out = _dot(mixed.astype(BF16), wo_ref[...]) + bo_ref[...]
    out_ref[...] = _layer_norm(alpha * x + out, lng_ref[...], lnb_ref[...]).astype(out_ref.dtype)


def _merge(x, ya, yb, oc, wg, bg, woc, wo, bo, lng, lnb, *, alpha):
    batch, seq, d = x.shape
    tile = MERGE_TILE
    rows = pl.BlockSpec((None, tile, d), lambda b, i: (b, i, 0))
    return pl.pallas_call(
        functools.partial(_merge_kernel, alpha=alpha),
        grid=(batch, seq // tile),
        in_specs=[rows, rows, rows, rows]
        + [_full_spec(a.shape) for a in (wg, bg, woc, wo, bo, lng, lnb)],
        out_specs=rows,
        out_shape=jax.ShapeDtypeStruct((batch, seq, d), F32),
        compiler_params=pltpu.CompilerParams(
            dimension_semantics=("arbitrary", "arbitrary"), vmem_limit_bytes=VMEM_LIMIT),
        name="merge_out",
    )(x, ya, yb, oc, wg, bg, woc, wo, bo, lng, lnb)


def _block_diag_gates(wr, wi):
    per_tile = MXU_DIM // wr.shape[1]
    eye = jnp.eye(per_tile, dtype=wr.dtype)

    def bd(w):
        w4 = w.reshape(-1, per_tile, w.shape[1], w.shape[2])
        return jnp.einsum("gaij,ab->gaibj", w4, eye).reshape(-1, MXU_DIM, MXU_DIM)

    return jnp.concatenate([bd(wr), bd(wi)], axis=-1).astype(BF16)


def kernel(x, w_in, b_in, conv_a_w, conv_a_b, lru_wr, lru_br, lru_wi, lru_bi, lru_lambda,
           w_out_a, conv_b_w, conv_b_b, lnb_g, lnb_b, w_out_b, w_out_c, w_o, b_o, ln_g, ln_b):
    batch, seq, d = x.shape
    depth = w_in.shape[0]
    assert batch == SUBLANES and d == N_HEADS * HEAD_DIM
    alpha = float((2 * depth) ** 0.25)
    row = lambda a: a.reshape(1, -1)

    o_gc = 8 * d + N_HEADS
    w_bf = w_in.astype(BF16)
    b_3d = b_in.reshape(depth, 1, -1)

    for l in range(depth):
        ya = _branch_a(x, w_bf, b_3d, l, conv_a_w[l], row(conv_a_b[l]),
                       _block_diag_gates(lru_wr[l], lru_wi[l]), row(lru_br[l]), row(lru_bi[l]),
                       row(lru_lambda[l]), w_out_a[l].astype(BF16))
        yb, q, k, v, cs = _branch_bc(x, w_bf, b_3d, l, conv_b_w[l], row(conv_b_b[l]),
                                     row(lnb_g[l]), row(lnb_b[l]), w_out_b[l].astype(BF16))
        oc = _attention(q, k, v, cs)

        x = _merge(x, ya, yb, oc, w_bf[l, :, o_gc:], row(b_in[l, o_gc:]),
                   w_out_c[l].astype(BF16), w_o[l].astype(BF16), row(b_o[l]),
                   row(ln_g[l]), row(ln_b[l]), alpha=alpha)
    return x
```

```python
import functools
import math

import jax
import jax.numpy as jnp
import numpy as np
from jax import lax
from jax.experimental import pallas as pl
from jax.experimental.pallas import tpu as pltpu

F32 = jnp.float32
BF16 = jnp.bfloat16

LRU_C = 8.0
LN_EPS = 1e-5
CONV_A = 4
CONV_B = 31
N_HEADS = 16
HEAD_DIM = 64
LANES = 128
SUBLANES = 8
MXU_DIM = 256
VMEM_LIMIT = 56 * 1024 * 1024
LOG2E = math.log2(math.e)
MASKED = -1e30
SPLIT = 3
QUERY_LANES = 64

T_TILE = 64
CONV_CHUNK = 64
Q_TILE = 256
KEY_CHUNK = 128
AHEAD = 3
N_BUFFERS = 4
PAIRS_PER_STEP = 2
A_T_TILE = 128
STAGE3_AFTER_GROUP = 1
MERGE_TILE = 512


def _sigmoid(x):
    return 0.5 * jnp.tanh(0.5 * x) + 0.5


def _silu(x):
    return x * _sigmoid(x)


def _log_sigmoid(x):
    return jnp.minimum(x, 0.0) - jnp.log1p(jnp.exp(-jnp.abs(x)))


def _layer_norm(x, g, b):
    mu = jnp.mean(x, axis=-1, keepdims=True)
    xc = x - mu
    var = jnp.mean(xc * xc, axis=-1, keepdims=True)
    return xc * lax.rsqrt(var + LN_EPS) * g + b


def _dot(a, b):
    return jnp.dot(a, b, preferred_element_type=F32)


def _lane_block(c):
    return slice(c * LANES, (c + 1) * LANES)


def _store_time_major(dst, row0, val, batch, t):
    for c in range(dst.shape[0]):
        for b in range(batch):
            dst[c, pl.ds(row0 + b, t, stride=batch), :] = val[b * t:(b + 1) * t, _lane_block(c)]


def _load_batch_major(src, batch, t):
    return jnp.concatenate(
        [jnp.concatenate([src[c, pl.ds(b, t, stride=batch), :] for c in range(src.shape[0])], axis=-1)
         for b in range(batch)], axis=0)


def _full_spec(shape):
    return pl.BlockSpec(shape, lambda *_: (0,) * len(shape))


def _col_spec(arr, layer, cols, block):
    return pl.BlockSpec((None, arr.shape[1], cols), lambda *_: (layer, 0, block))


def _seq_params(n_inputs, fusable):
    return pltpu.CompilerParams(
        dimension_semantics=("arbitrary",), vmem_limit_bytes=VMEM_LIMIT,
        allow_input_fusion=[i in fusable for i in range(n_inputs)])


def _branch_a_kernel(x_ref, w_ref, b_ref, cw_ref, cb_ref, wg_ref, br_ref, bi_ref, lam_ref,
                     wo_ref, o_ref, xbuf, a_buf, h_buf, hstate, *, t, batch):
    rows = t * batch
    hist = (CONV_A - 1) * batch
    slabs = xbuf.shape[0]
    width = slabs * LANES

    @pl.when(pl.program_id(0) == 0)
    def _():
        xbuf[:, 0:hist, :] = jnp.zeros((slabs, hist, LANES), F32)
        hstate[...] = jnp.zeros_like(hstate)

    xb = x_ref[...].reshape(rows, x_ref.shape[2]).astype(BF16)
    p = _dot(xb, w_ref[...]) + b_ref[...]
    ga = p[:, width:]
    _store_time_major(xbuf, hist, p[:, :width], batch, t)

    ys = []
    for c in range(slabs):
        lanes = _lane_block(c)
        y = cb_ref[:, lanes] + cw_ref[0:1, lanes] * xbuf[c, 0:rows, :]
        for j in range(1, CONV_A):
            y = y + cw_ref[j:j + 1, lanes] * xbuf[c, j * batch:j * batch + rows, :]
        xbuf[c, 0:hist, :] = xbuf[c, rows:rows + hist, :]
        ys.append(y)

    log_sig = _log_sigmoid(lam_ref[...])
    per_tile = MXU_DIM // LANES
    for g in range(width // MXU_DIM):
        yb = jnp.concatenate(ys[g * per_tile:(g + 1) * per_tile], axis=-1).astype(BF16)
        gates = _dot(yb, wg_ref[g])
        for s in range(per_tile):
            c = g * per_tile + s
            lanes = _lane_block(c)
            r = _sigmoid(gates[:, _lane_block(s)] + br_ref[:, lanes])
            ig = _sigmoid(gates[:, _lane_block(per_tile + s)] + bi_ref[:, lanes])
            log_a = (LRU_C * r) * log_sig[:, lanes]
            a = jnp.exp(log_a)
            gain = jnp.sqrt(-jnp.tanh(log_a) * (1.0 + a * a))
            a_buf[c] = a
            h_buf[c] = gain * (ig * ys[c])

    def step(tt, hs):
        r0 = pl.multiple_of(tt * batch, batch)
        out = []
        for c in range(slabs):
            h = a_buf[c, pl.ds(r0, batch), :] * hs[c] + h_buf[c, pl.ds(r0, batch), :]
            h_buf[c, pl.ds(r0, batch), :] = h
            out.append(h)
        return tuple(out)

    hs = lax.fori_loop(0, t, step, tuple(hstate[c] for c in range(slabs)), unroll=8)
    for c in range(slabs):
        hstate[c] = hs[c]

    ha = _load_batch_major(h_buf, batch, t) * _silu(ga)
    o_ref[...] = _dot(ha.astype(BF16), wo_ref[...]).astype(o_ref.dtype).reshape(o_ref.shape)


def _branch_a(x, w_in, b_in, layer, cw, cb, wg, br, bi, lam, wo):
    batch, seq, d = x.shape
    width = cw.shape[1]
    slabs = width // LANES
    t = A_T_TILE
    rows = t * batch
    return pl.pallas_call(
        functools.partial(_branch_a_kernel, t=t, batch=batch),
        grid=(seq // t,),
        in_specs=[pl.BlockSpec((batch, t, d), lambda i: (0, i, 0)),
                  _col_spec(w_in, layer, 2 * width, 0), _col_spec(b_in, layer, 2 * width, 0)]
        + [_full_spec(a.shape) for a in (cw, cb, wg, br, bi, lam, wo)],
        out_specs=pl.BlockSpec((batch, t, d), lambda i: (0, i, 0)),
        out_shape=jax.ShapeDtypeStruct((batch, seq, d), BF16),
        scratch_shapes=[
            pltpu.VMEM((slabs, (CONV_A - 1) * batch + rows, LANES), F32),
            pltpu.VMEM((slabs, rows, LANES), F32),
            pltpu.VMEM((slabs, rows, LANES), F32),
            pltpu.VMEM((slabs, batch, LANES), F32),
        ],
        compiler_params=_seq_params(10, fusable={1}),
        name="branch_a",
    )(x, w_in, b_in, cw, cb, wg, br, bi, lam, wo)


def _gate_placement():
    p = np.zeros((SPLIT * LANES, LANES), np.float32)
    for hd in range(N_HEADS):
        for e in range(SPLIT):
            p[e * LANES + hd, SPLIT * hd + e] = -1.0
            p[e * LANES + hd, QUERY_LANES + SPLIT * hd + e] = 1.0
    return jnp.asarray(p, BF16)


def _branch_bc_kernel(x_ref, wu_ref, wgate_ref, wgb_ref, wq_ref, wk_ref, wv_ref, wf_ref,
                      bu_ref, bgate_ref, bgb_ref, bq_ref, bk_ref, bv_ref, bf_ref,
                      cw_ref, cb_ref, lng_ref, lnb_ref, wo_ref, place_ref,
                      o_ref, q_ref, k_ref, v_ref, cs_ref,
                      ubuf, vbuf, gbuf, hbuf, wrep, cbuf, cstate, cbefore, *, t, batch, n_tiles):
    i = pl.program_id(0)
    rows = t * batch
    hist = (CONV_B - 1) * batch
    slot_rows = hist + rows
    slabs = ubuf.shape[0]
    width = slabs * LANES
    reps = CONV_CHUNK // SUBLANES
    per_group = MXU_DIM // LANES
    cur = lax.rem(i, 2)
    prev = 1 - cur
    cur0 = pl.multiple_of(cur * slot_rows, SUBLANES)
    prev0 = pl.multiple_of(prev * slot_rows, SUBLANES)

    @pl.when(i == 0)
    def _():
        ubuf[...] = jnp.zeros_like(ubuf)
        gbuf[...] = jnp.zeros_like(gbuf)
        hbuf[...] = jnp.zeros_like(hbuf)
        cstate[...] = jnp.zeros_like(cstate)
        cbefore[...] = jnp.zeros_like(cbefore)
        for j in range(CONV_B):
            wrep[j] = jnp.broadcast_to(cw_ref[j:j + 1, :], (SUBLANES, width))
        wrep[CONV_B] = jnp.broadcast_to(cb_ref[...], (SUBLANES, width))

    xb = x_ref[...].reshape(rows, x_ref.shape[2]).astype(BF16)

    def proj(w, b, g):
        sl = slice(g * MXU_DIM, (g + 1) * MXU_DIM)
        return _dot(xb, w[:, sl]) + b[:, sl]

    def conv_slab(c):
        lanes = _lane_block(c)
        for r0 in range(0, rows, CONV_CHUNK):
            acc = jnp.concatenate([wrep[CONV_B, :, lanes]] * reps, axis=0)
            for j in range(CONV_B):
                acc = acc + jnp.concatenate([wrep[j, :, lanes]] * reps, axis=0) * ubuf[
                    c, pl.ds(prev0 + r0 + j * batch, CONV_CHUNK), :]
            vbuf[c, r0:r0 + CONV_CHUNK, :] = acc

    def fill_slab(c, u):
        for bb in range(batch):
            ubuf[c, pl.ds(cur0 + hist + bb, t, stride=batch), :] = u[bb * t:(bb + 1) * t, :]
        ubuf[c, pl.ds(cur0, hist), :] = ubuf[c, pl.ds(prev0 + rows, hist), :]

    def qkv_group(ref, w, b, g, scale):
        val = proj(w, b, g)
        if scale != 1.0:
            val = val * scale
        ref[:, :, g * MXU_DIM:(g + 1) * MXU_DIM] = val.astype(ref.dtype).reshape(batch, t, MXU_DIM)

    q_scale = LOG2E / math.sqrt(HEAD_DIM)
    for g in range(width // MXU_DIM):
        conv_slab(g * per_group)
        u = proj(wu_ref, bu_ref, g) * _sigmoid(proj(wgate_ref, bgate_ref, g))
        gbuf[cur, :, g * MXU_DIM:(g + 1) * MXU_DIM] = 0.5 * proj(wgb_ref, bgb_ref, g)
        for s in range(per_group):
            fill_slab(g * per_group + s, u[:, _lane_block(s)])
        if g == STAGE3_AFTER_GROUP:
            o_ref[...] = _dot(hbuf[...], wo_ref[...]).astype(o_ref.dtype).reshape(o_ref.shape)
        qkv_group(q_ref, wq_ref, bq_ref, g, q_scale)
        conv_slab(g * per_group + 1)
        qkv_group(k_ref, wk_ref, bk_ref, g, 1.0)
        qkv_group(v_ref, wv_ref, bv_ref, g, 1.0)

    lf = _log_sigmoid(_dot(xb, wf_ref[...]) + bf_ref[...]) * LOG2E
    head_lane = lax.broadcasted_iota(jnp.int32, (1, LANES), 1) < N_HEADS
    lf = jnp.where(head_lane, lf, 0.0)
    _store_time_major(cbuf, 0, lf, batch, t)
    start = jnp.where(i < n_tiles, cstate[...], cbefore[...])
    cbefore[...] = start
    c = start
    for tt in range(t):
        c = c + cbuf[0, tt * batch:(tt + 1) * batch, :]
        cbuf[0, tt * batch:(tt + 1) * batch, :] = c
    cstate[...] = c
    rem = _load_batch_major(cbuf, batch, t)
    pieces = []
    for _ in range(SPLIT):
        piece = rem.astype(BF16)
        pieces.append(piece)
        rem = rem - piece.astype(F32)
    cs = _dot(jnp.concatenate(pieces, axis=-1), place_ref[...])
    cs_ref[...] = cs.astype(cs_ref.dtype).reshape(cs_ref.shape)

    yh = _layer_norm(_load_batch_major(vbuf, batch, t), 0.5 * lng_ref[...], 0.5 * lnb_ref[...])
    gh = gbuf[prev]
    hbuf[...] = ((yh * (1.0 + jnp.tanh(yh))) * (gh * (1.0 + jnp.tanh(gh)))).astype(hbuf.dtype)


def _branch_bc(x, w_in, b_in, layer, cw, cb, lng, lnb, wo):
    batch, seq, d = x.shape
    width = cw.shape[1]
    assert width == d and (2 + 6) * width % LANES == 0
    f_block = (2 + 6) * width // LANES
    weights = [_col_spec(w_in, layer, width, 2 + j) for j in range(6)]
    weights.append(_col_spec(w_in, layer, LANES, f_block))
    biases = [_col_spec(b_in, layer, width, 2 + j) for j in range(6)]
    biases.append(_col_spec(b_in, layer, LANES, f_block))
    slabs = width // LANES
    t = T_TILE
    rows = t * batch
    n = seq // t
    assert rows >= (CONV_B - 1) * batch and rows % CONV_CHUNK == 0
    place = _gate_placement()
    now = lambda cols: pl.BlockSpec((batch, t, cols), lambda i: (0, jnp.minimum(i, n - 1), 0))
    lag = lambda cols: pl.BlockSpec((batch, t, cols), lambda i: (0, jnp.maximum(i - 2, 0), 0))
    act = lambda cols: jax.ShapeDtypeStruct((batch, seq, cols), BF16)
    return pl.pallas_call(
        functools.partial(_branch_bc_kernel, t=t, batch=batch, n_tiles=n),
        grid=(n + 2,),
        in_specs=[now(d)] + weights + biases
        + [_full_spec(a.shape) for a in (cw, cb, lng, lnb, wo, place)],
        out_specs=[lag(d), now(width), now(width), now(width), now(LANES)],
        out_shape=[act(d), act(width), act(width), act(width), act(LANES)],
        scratch_shapes=[
            pltpu.VMEM((slabs, 2 * ((CONV_B - 1) * batch + rows), LANES), F32),
            pltpu.VMEM((slabs, rows, LANES), F32),
            pltpu.VMEM((2, rows, width), F32),
            pltpu.VMEM((rows, width), BF16),
            pltpu.VMEM((CONV_B + 1, SUBLANES, width), F32),
            pltpu.VMEM((1, rows, LANES), F32),
            pltpu.VMEM((batch, LANES), F32),
            pltpu.VMEM((batch, LANES), F32),
        ],
        compiler_params=_seq_params(21, fusable=set(range(1, 8))),
        name="branch_bc",
    )(x, *([w_in] * 7), *([b_in] * 7), cw, cb, lng, lnb, wo, place)


def _attn_kernel(q_ref, k_ref, v_ref, cs_ref, o_ref, qcat_t, kcat, vm_t, *bufs, tile):
    group = pl.program_id(1)
    seq = q_ref.shape[0]
    pairs = q_ref.shape[1] // LANES
    lane = lax.broadcasted_iota(jnp.int32, (1, LANES), 1)
    sub = lax.broadcasted_iota(jnp.int32, (LANES, 1), 0)
    head_rows = (sub < HEAD_DIM, sub >= HEAD_DIM)
    sum_row = (HEAD_DIM, 0)
    zero = jnp.zeros((), BF16)
    one = jnp.ones((), BF16)
    keep = (lax.broadcasted_iota(jnp.int32, (tile, tile), 0)
            <= lax.broadcasted_iota(jnp.int32, (tile, tile), 1))

    cs = cs_ref[...]
    ct = cs.T
    for pp in range(pairs):
        lanes = _lane_block(pp)
        qt, vt = q_ref[:, lanes].T, v_ref[:, lanes].T
        kcat[pp, :, 0:LANES] = k_ref[:, lanes]
        for h in range(2):
            lo_q = SPLIT * (2 * (pairs * group + pp) + h)
            lo_k = QUERY_LANES + lo_q
            sel_q = (sub >= lo_q) & (sub < lo_q + SPLIT)
            sel_k = (lane >= lo_k) & (lane < lo_k + SPLIT)
            qcat_t[2 * pp + h, 0:LANES, :] = jnp.where(head_rows[h], qt, zero)
            qcat_t[2 * pp + h, LANES:, :] = jnp.where(
                sel_q, one, jnp.where(sub >= QUERY_LANES, ct, zero))
            kcat[pp, :, (h + 1) * LANES:(h + 2) * LANES] = jnp.where(
                sel_k, one, jnp.where(lane < QUERY_LANES, cs, zero))
            vm_t[2 * pp + h] = jnp.where(
                sub == sum_row[h], one, jnp.where(head_rows[h], vt, zero))

    units = [(pp, i, h) for pp in range(pairs) for i in range(seq // tile) for h in range(2)]

    def buffers(unit):
        slot = units.index(unit) % N_BUFFERS
        return bufs[slot], bufs[N_BUFFERS + slot]

    def scores(unit):
        pp, i, h = unit
        q0, klen = i * tile, (i + 1) * tile
        s_buf, _ = buffers(unit)
        keys = jnp.concatenate(
            [kcat[pp, 0:klen, 0:LANES], kcat[pp, 0:klen, (h + 1) * LANES:(h + 2) * LANES]], axis=-1)
        s_buf[0:klen, :] = _dot(keys, qcat_t[2 * pp + h, :, q0:klen])
        s_buf[q0:klen, :] = jnp.where(keep, s_buf[q0:klen, :], MASKED)

    def probs_and_values(unit):
        pp, i, h = unit
        klen = (i + 1) * tile
        s_buf, p_buf = buffers(unit)
        m8 = None
        for c in range(klen // KEY_CHUNK):
            rows = slice(c * KEY_CHUNK, (c + 1) * KEY_CHUNK)
            part = jnp.max(s_buf[rows, :].reshape(KEY_CHUNK // SUBLANES, SUBLANES, tile), axis=0)
            m8 = part if m8 is None else jnp.maximum(m8, part)
        m = jnp.max(m8, axis=0, keepdims=True)
        for c in range(klen // KEY_CHUNK):
            rows = slice(c * KEY_CHUNK, (c + 1) * KEY_CHUNK)
            p_buf[rows, :] = jnp.exp2(s_buf[rows, :] - m).astype(BF16)
        pv = _dot(vm_t[2 * pp + h, :, 0:klen], p_buf[0:klen, :])
        return pv / pv[sum_row[h]:sum_row[h] + 1, :]

    for unit in units[:AHEAD]:
        scores(unit)
    outs = []
    for u, unit in enumerate(units):
        if u + AHEAD < len(units):
            scores(units[u + AHEAD])
        outs.append(probs_and_values(unit))
        pp, i, h = unit
        if h == 1:
            o_ref[i * tile:(i + 1) * tile, _lane_block(pp)] = (
                jnp.where(head_rows[0], outs[0], outs[1]).T.astype(o_ref.dtype))
            outs = []


def _attention(q, k, v, cs):
    batch, seq, width = q.shape
    tile = Q_TILE
    assert tile % KEY_CHUNK == 0 and AHEAD < N_BUFFERS
    assert QUERY_LANES + SPLIT * N_HEADS <= LANES and SPLIT * N_HEADS <= QUERY_LANES
    pairs = PAIRS_PER_STEP
    block = pl.BlockSpec((None, seq, pairs * LANES), lambda b, g: (b, 0, g))
    return pl.pallas_call(
        functools.partial(_attn_kernel, tile=tile),
        grid=(batch, width // (pairs * LANES)),
        in_specs=[block, block, block, pl.BlockSpec((None, seq, LANES), lambda b, g: (b, 0, 0))],
        out_specs=block,
        out_shape=jax.ShapeDtypeStruct((batch, seq, width), BF16),
        scratch_shapes=[
            pltpu.VMEM((2 * pairs, 2 * LANES, seq), BF16),
            pltpu.VMEM((pairs, seq, 3 * LANES), BF16),
            pltpu.VMEM((2 * pairs, LANES, seq), BF16),
        ] + [pltpu.VMEM((seq, tile), F32)] * N_BUFFERS + [pltpu.VMEM((seq, tile), BF16)] * N_BUFFERS,
        compiler_params=pltpu.CompilerParams(
            dimension_semantics=("arbitrary", "arbitrary"), vmem_limit_bytes=VMEM_LIMIT),
        name="fox_attention",
    )(q, k, v, cs)


def _merge_kernel(x_ref, ya_ref, yb_ref, oc_ref, wg_ref, bg_ref, woc_ref, wo_ref, bo_ref,
                  lng_ref, lnb_ref, out_ref, *, alpha):
    d = x_ref.shape[1]
    x = x_ref[...]
    pg = _dot(x.astype(BF16), wg_ref[...]) + bg_ref[...]
    hc = oc_ref[...].astype(F32) * _silu(pg[:, :d])
    yc = _dot(hc.astype(BF16), woc_ref[...])
    mixed = (_sigmoid(pg[:, d:2 * d]) * ya_ref[...].astype(F32)
             + _sigmoid(pg[:, 2 * d:3 * d]) * yb_ref[...].astype(F32)
             + _sigmoid(pg[:, 3 * d:]) * yc)
    out = _dot(mixed.astype(BF16), wo_ref[...]) + bo_ref[...]
    out_ref[...] = _layer_norm(alpha * x + out, lng_ref[...], lnb_ref[...]).astype(out_ref.dtype)


def _merge(x, ya, yb, oc, wg, bg, woc, wo, bo, lng, lnb, *, alpha):
    batch, seq, d = x.shape
    tile = MERGE_TILE
    rows = pl.BlockSpec((None, tile, d), lambda b, i: (b, i, 0))
    return pl.pallas_call(
        functools.partial(_merge_kernel, alpha=alpha),
        grid=(batch, seq // tile),
        in_specs=[rows, rows, rows, rows]
        + [_full_spec(a.shape) for a in (wg, bg, woc, wo, bo, lng, lnb)],
        out_specs=rows,
        out_shape=jax.ShapeDtypeStruct((batch, seq, d), F32),
        compiler_params=pltpu.CompilerParams(
            dimension_semantics=("arbitrary", "arbitrary"), vmem_limit_bytes=VMEM_LIMIT),
        name="merge_out",
    )(x, ya, yb, oc, wg, bg, woc, wo, bo, lng, lnb)


def _block_diag_gates(wr, wi):
    per_tile = MXU_DIM // wr.shape[1]
    eye = jnp.eye(per_tile, dtype=wr.dtype)

    def bd(w):
        w4 = w.reshape(-1, per_tile, w.shape[1], w.shape[2])
        return jnp.einsum("gaij,ab->gaibj", w4, eye).reshape(-1, MXU_DIM, MXU_DIM)

    return jnp.concatenate([bd(wr), bd(wi)], axis=-1).astype(BF16)


def kernel(x, w_in, b_in, conv_a_w, conv_a_b, lru_wr, lru_br, lru_wi, lru_bi, lru_lambda,
           w_out_a, conv_b_w, conv_b_b, lnb_g, lnb_b, w_out_b, w_out_c, w_o, b_o, ln_g, ln_b):
    batch, seq, d = x.shape
    depth = w_in.shape[0]
    assert batch == SUBLANES and d == N_HEADS * HEAD_DIM
    alpha = float((2 * depth) ** 0.25)
    row = lambda a: a.reshape(1, -1)

    o_gc = 8 * d + N_HEADS
    w_bf = w_in.astype(BF16)
    b_3d = b_in.reshape(depth, 1, -1)

    for l in range(depth):
        ya = _branch_a(x, w_bf, b_3d, l, conv_a_w[l], row(conv_a_b[l]),
                       _block_diag_gates(lru_wr[l], lru_wi[l]), row(lru_br[l]), row(lru_bi[l]),
                       row(lru_lambda[l]), w_out_a[l].astype(BF16))
        yb, q, k, v, cs = _branch_bc(x, w_bf, b_3d, l, conv_b_w[l], row(conv_b_b[l]),
                                     row(lnb_g[l]), row(lnb_b[l]), w_out_b[l].astype(BF16))
        oc = _attention(q, k, v, cs)

        x = _merge(x, ya, yb, oc, w_bf[l, :, o_gc:], row(b_in[l, o_gc:]),
                   w_out_c[l].astype(BF16), w_o[l].astype(BF16), row(b_o[l]),
                   row(ln_g[l]), row(ln_b[l]), alpha=alpha)
    return x
```
